```python
import jax
import jax.numpy as jnp
from jax import lax
import numpy as np

D_MODEL = 2048
BATCH = 2
SEQ = 8192
DEPTH = 4

GRID_W = 64
CTX_LEN = 256
N_MOD = 9
D_FF = 5632
D_A = 1024
CONV_A = 31
HB = 8
Q_LORA = 512
KV_LORA = 512
NOPE_B = 128
ROPE_B = 64
V_B = 128
QK_B = NOPE_B + ROPE_B
HC = 8
HKV_C = 2
DH_C = 128
GROUP_C = HC // HKV_C
D_D = 1024
CONV_D = 3
N_BRANCH = 4
Q_BLOCK = 128
ROPE_BASE = 10000.0
EPS = 1e-6
SPLIT_SIZES = (KV_LORA, ROPE_B, HKV_C * DH_C, HKV_C * DH_C, Q_LORA, HC * DH_C,
               2 * D_A, 3 * D_D, N_BRANCH * D_MODEL)
N_CTX_KV_PIECES = 4
KV_COLS = KV_LORA + ROPE_B + 2 * HKV_C * DH_C
IN_COLS = KV_COLS + Q_LORA + HC * DH_C + 2 * D_A + 3 * D_D + N_BRANCH * D_MODEL

kernel_name = "hybrid_parallel_mixer_dit"


def _split(p, sizes):
    out, start = [], 0
    for size in sizes:
        out.append(p[..., start:start + size])
        start += size
    return out


def _rms(x, gain=None):
    xf = x.astype(jnp.float32)
    y = (xf * lax.rsqrt(jnp.mean(xf * xf, axis=-1, keepdims=True) + EPS)).astype(x.dtype)
    return y if gain is None else y * gain


def _layer_norm(x, gain, bias):
    xf = x.astype(jnp.float32)
    mu = jnp.mean(xf, axis=-1, keepdims=True)
    var = jnp.mean(jnp.square(xf - mu), axis=-1, keepdims=True)
    return ((xf - mu) * lax.rsqrt(var + EPS)).astype(x.dtype) * gain + bias


def _modulate(x, shift, scale):
    return _rms(x) * (1 + scale) + shift


def _swiglu(x, w_in, w_out):
    a, b = jnp.split(x @ w_in, 2, axis=-1)
    return (jax.nn.silu(a) * b) @ w_out


def _dwconv(x, w):
    return lax.conv_general_dilated(
        x, w[:, None, :], window_strides=(1,), padding="SAME",
        dimension_numbers=("NWC", "WIO", "NWC"), feature_group_count=x.shape[-1])


def _axial_rope_tables(n_tokens, d_rot):
    t = jnp.arange(n_tokens)
    row = (t // GRID_W).astype(jnp.float32)
    col = (t % GRID_W).astype(jnp.float32)
    n_freq = d_rot // 4
    inv = ROPE_BASE ** (-jnp.arange(n_freq, dtype=jnp.float32) / n_freq)
    ang = jnp.concatenate([row[:, None] * inv, col[:, None] * inv], axis=-1)
    return jnp.cos(ang), jnp.sin(ang)


def _apply_rope(x, rope):
    cos, sin = rope
    cos = cos[None, :, None, :].astype(x.dtype)
    sin = sin[None, :, None, :].astype(x.dtype)
    x1, x2 = jnp.split(x, 2, axis=-1)
    return jnp.concatenate([x1 * cos - x2 * sin, x2 * cos + x1 * sin], axis=-1)


def _attend(q, k, v):
    bsz, lq, hkv, grp, dk = q.shape
    nb = lq // Q_BLOCK
    qb = jnp.moveaxis(q.reshape(bsz, nb, Q_BLOCK, hkv, grp, dk), 1, 0)

    def one_block(qblk):
        s = jnp.einsum("bqhgd,bkhd->bhgqk", qblk, k, preferred_element_type=jnp.float32)
        p = jax.nn.softmax(s, axis=-1).astype(v.dtype)
        return jnp.einsum("bhgqk,bkhd->bqhgd", p, v)

    o = lax.map(one_block, qb)
    return jnp.moveaxis(o, 0, 1).reshape(bsz, lq, hkv * grp * v.shape[-1])


def _mla_kv(ckv, krope, lp, rope):
    bsz, n, _ = ckv.shape
    kv = (_rms(ckv, lp["g_kv_lora"]) @ lp["w_ukv"]).reshape(bsz, n, HB, NOPE_B + V_B)
    k_nope, v = kv[..., :NOPE_B], kv[..., NOPE_B:]
    k_pe = jnp.broadcast_to(krope[:, :, None, :], (bsz, n, HB, ROPE_B))
    k = _rms(jnp.concatenate([k_nope, k_pe], axis=-1), lp["g_k_b"])
    if rope is not None:
        k = jnp.concatenate([k[..., :NOPE_B], _apply_rope(k[..., NOPE_B:], rope)], axis=-1)
    return k, v


def _mla_q(cq, lp, rope):
    bsz, n, _ = cq.shape
    q = (_rms(cq, lp["g_q_lora"]) @ lp["w_uq"]).reshape(bsz, n, HB, QK_B)
    q = _rms(q, lp["g_q_b"])
    if rope is not None:
        q = jnp.concatenate([q[..., :NOPE_B], _apply_rope(q[..., NOPE_B:], rope)], axis=-1)
    return (q * (QK_B ** -0.5))[:, :, :, None, :]


def _gqa_kv(k, v, lp, rope):
    bsz, n, _ = k.shape
    k = _rms(k.reshape(bsz, n, HKV_C, DH_C), lp["g_k_c"])
    if rope is not None:
        k = _apply_rope(k, rope)
    return k, v.reshape(bsz, n, HKV_C, DH_C)


def _gqa_q(q, lp, rope):
    bsz, n, _ = q.shape
    q = _rms(q.reshape(bsz, n, HC, DH_C), lp["g_q_c"])
    if rope is not None:
        q = _apply_rope(q, rope)
    return (q * (DH_C ** -0.5)).reshape(bsz, n, HKV_C, GROUP_C, DH_C)


def _conformer_conv(u, lp):
    a = u[..., :D_A] * jax.nn.sigmoid(u[..., D_A:])
    a = _dwconv(a, lp["w_dw_a"]) + lp["b_dw_a"]
    a = jax.nn.silu(_layer_norm(a, lp["g_ln_a"], lp["b_ln_a"]))
    return a @ lp["w_out_a"]


def _short_conv(u, lp):
    bg, cg, h = jnp.split(u, 3, axis=-1)
    return (bg * _dwconv(cg * h, lp["w_dw_d"])) @ lp["w_out_d"]


def _merge(gate_logits, ys, w_o):
    g = jax.nn.sigmoid(gate_logits).reshape(gate_logits.shape[:-1] + (N_BRANCH, D_MODEL))
    y = g[..., 0, :] * ys[0]
    for i in range(1, N_BRANCH):
        y = y + g[..., i, :] * ys[i]
    return y @ w_o


def _token_mixer(xl, xc, lp, rope_b, rope_c, ctx_out):
    (ckv_l, kr_l, k_l, v_l, cq_l, q_l, ua_l, ud_l, gt_l) = _split(xl @ lp["w_in"], SPLIT_SIZES)
    if ctx_out:
        pieces_c = _split(xc @ lp["w_in"], SPLIT_SIZES)
    else:
        pieces_c = _split(xc @ lp["w_in"][:, :KV_COLS], SPLIT_SIZES[:N_CTX_KV_PIECES])
    ckv_c, kr_c, k_c, v_c = pieces_c[:N_CTX_KV_PIECES]

    kb_c, vb_c = _mla_kv(ckv_c, kr_c, lp, None)
    kb_l, vb_l = _mla_kv(ckv_l, kr_l, lp, rope_b)
    kc_c, vc_c = _gqa_kv(k_c, v_c, lp, None)
    kc_l, vc_l = _gqa_kv(k_l, v_l, lp, rope_c)

    yb_l = _attend(_mla_q(cq_l, lp, rope_b),
                   jnp.concatenate([kb_c, kb_l], axis=1),
                   jnp.concatenate([vb_c, vb_l], axis=1)) @ lp["w_o_b"]
    yc_l = _attend(_gqa_q(q_l, lp, rope_c),
                   jnp.concatenate([kc_c, kc_l], axis=1),
                   jnp.concatenate([vc_c, vc_l], axis=1)) @ lp["w_o_c"]
    ya_l = _conformer_conv(ua_l, lp)
    yd_l = _short_conv(ud_l, lp)
    y_l = _merge(gt_l, (ya_l, yb_l, yc_l, yd_l), lp["w_o"])
    if not ctx_out:
        return y_l, None

    cq_c, q_c, ua_c, ud_c, gt_c = pieces_c[N_CTX_KV_PIECES:]
    yb_c = _attend(_mla_q(cq_c, lp, None), kb_c, vb_c) @ lp["w_o_b"]
    yc_c = _attend(_gqa_q(q_c, lp, None), kc_c, vc_c) @ lp["w_o_c"]
    ya_c = _conformer_conv(ua_c, lp)
    yd_c = _short_conv(ud_c, lp)
    y_c = _merge(gt_c, (ya_c, yb_c, yc_c, yd_c), lp["w_o"])
    return y_l, y_c


def setup_inputs(seed: int = 0) -> dict:
    key = jax.random.key(seed)
    keys = jax.random.split(key, 32)
    counter = [0]

    def nrm(shape, scale):
        k = keys[counter[0]]
        counter[0] += 1
        return jax.random.normal(k, shape, jnp.float32) * scale

    L = DEPTH
    D = D_MODEL
    return {
        "x": nrm((BATCH, SEQ, D), 1.0),
        "c": nrm((BATCH, D), 1.0),
        "ctx": nrm((BATCH, CTX_LEN, D), 1.0),
        "c_ctx": nrm((D,), 1.0),
        "w_mod": nrm((L, D, N_MOD * D), 0.5 * D ** -0.5),
        "b_mod": nrm((L, N_MOD * D), 0.01),
        "w_ffn1_in": nrm((L, D, 2 * D_FF), D ** -0.5),
        "w_ffn1_out": nrm((L, D_FF, D), D_FF ** -0.5),
        "w_ffn2_in": nrm((L, D, 2 * D_FF), D ** -0.5),
        "w_ffn2_out": nrm((L, D_FF, D), D_FF ** -0.5),
        "w_in": nrm((L, D, IN_COLS), D ** -0.5),
        "g_q_lora": 1.0 + nrm((L, Q_LORA), 0.02),
        "w_uq": nrm((L, Q_LORA, HB * QK_B), Q_LORA ** -0.5),
        "g_kv_lora": 1.0 + nrm((L, KV_LORA), 0.02),
        "w_ukv": nrm((L, KV_LORA, HB * (NOPE_B + V_B)), KV_LORA ** -0.5),
        "g_q_b": 1.0 + nrm((L, QK_B), 0.02),
        "g_k_b": 1.0 + nrm((L, QK_B), 0.02),
        "w_o_b": nrm((L, HB * V_B, D), (HB * V_B) ** -0.5),
        "g_q_c": 1.0 + nrm((L, DH_C), 0.02),
        "g_k_c": 1.0 + nrm((L, DH_C), 0.02),
        "w_o_c": nrm((L, HC * DH_C, D), (HC * DH_C) ** -0.5),
        "w_dw_a": nrm((L, CONV_A, D_A), CONV_A ** -0.5),
        "b_dw_a": nrm((L, D_A), 0.01),
        "g_ln_a": 1.0 + nrm((L, D_A), 0.02),
        "b_ln_a": nrm((L, D_A), 0.01),
        "w_out_a": nrm((L, D_A, D), D_A ** -0.5),
        "w_dw_d": nrm((L, CONV_D, D_D), CONV_D ** -0.5),
        "w_out_d": nrm((L, D_D, D), D_D ** -0.5),
        "w_o": nrm((L, D, D), D ** -0.5),
    }


def reference(x, c, ctx, c_ctx, w_mod, b_mod, w_ffn1_in, w_ffn1_out, w_ffn2_in, w_ffn2_out,
              w_in, g_q_lora, w_uq, g_kv_lora, w_ukv, g_q_b, g_k_b, w_o_b, g_q_c, g_k_c, w_o_c,
              w_dw_a, b_dw_a, g_ln_a, b_ln_a, w_out_a, w_dw_d, w_out_d, w_o):
    bsz, n_lat, _ = x.shape
    rope_b = _axial_rope_tables(n_lat, ROPE_B)
    rope_c = _axial_rope_tables(n_lat, DH_C)
    silu_c = jax.nn.silu(c)
    silu_cc = jax.nn.silu(c_ctx)
    hl, hc = x, ctx
    for i in range(DEPTH):
        last = i == DEPTH - 1
        lp = {
            "w_in": w_in[i], "g_q_lora": g_q_lora[i], "w_uq": w_uq[i],
            "g_kv_lora": g_kv_lora[i], "w_ukv": w_ukv[i], "g_q_b": g_q_b[i], "g_k_b": g_k_b[i],
            "w_o_b": w_o_b[i], "g_q_c": g_q_c[i], "g_k_c": g_k_c[i], "w_o_c": w_o_c[i],
            "w_dw_a": w_dw_a[i], "b_dw_a": b_dw_a[i], "g_ln_a": g_ln_a[i], "b_ln_a": b_ln_a[i],
            "w_out_a": w_out_a[i], "w_dw_d": w_dw_d[i], "w_out_d": w_out_d[i], "w_o": w_o[i],
        }
        mod_l = (silu_c @ w_mod[i] + b_mod[i]).reshape(bsz, N_MOD, 1, D_MODEL)
        ml = [mod_l[:, j] for j in range(N_MOD)]
        mc = (silu_cc @ w_mod[i] + b_mod[i]).reshape(N_MOD, D_MODEL)

        hl = hl + 0.5 * ml[2] * _swiglu(_modulate(hl, ml[0], ml[1]), w_ffn1_in[i], w_ffn1_out[i])
        hc = hc + 0.5 * mc[2] * _swiglu(_modulate(hc, mc[0], mc[1]), w_ffn1_in[i], w_ffn1_out[i])

        y_l, y_c = _token_mixer(_modulate(hl, ml[3], ml[4]), _modulate(hc, mc[3], mc[4]),
                                lp, rope_b, rope_c, not last)
        hl = hl + ml[5] * y_l

        hl = hl + 0.5 * ml[8] * _swiglu(_modulate(hl, ml[6], ml[7]), w_ffn2_in[i], w_ffn2_out[i])
        if not last:
            hc = hc + mc[5] * y_c
            hc = hc + 0.5 * mc[8] * _swiglu(_modulate(hc, mc[6], mc[7]), w_ffn2_in[i], w_ffn2_out[i])
    return hl
```

```python
import functools

import jax
import jax.numpy as jnp
from jax import lax
from jax.experimental import pallas as pl
from jax.experimental.pallas import tpu as pltpu

F32 = jnp.float32
BF16 = jnp.bfloat16

GRID_W = 64
N_MOD = 9
HB = 8
NOPE_B = 128
ROPE_B = 64
V_B = 128
QK_B = NOPE_B + ROPE_B
HC = 8
HKV_C = 2
DH_C = 128
GROUP_C = HC // HKV_C
CONV_A = 31
CONV_D = 3
N_BRANCH = 4
ROPE_BASE = 10000.0
EPS = 1e-6

LANES = 128
SUBLANES = 8
VMEM_LIMIT = 56 * 1024 * 1024

TM_A = 256
TM_F = 512
TF = 512
TN_MERGE = 256
TN_MOD = 1024
TN_MIX = 512
TK = 256
HALO_A = 16
HALO_D = 8
CONV_ROWS = 64
CONV_LANES = 128


def _cparams(sem):
    return pltpu.CompilerParams(dimension_semantics=sem, vmem_limit_bytes=VMEM_LIMIT)


def _pick_tile(n, cap):
    t = min(cap, n) // LANES * LANES
    while n % t:
        t -= LANES
    return t


def _sigmoid(x):
    return 1.0 / (1.0 + jnp.exp(-x))


def _modulated_norm(x, shift, scale):
    y = x * lax.rsqrt(jnp.mean(x * x, axis=-1, keepdims=True) + EPS)
    return y * (1.0 + scale) + shift


def _mod_kernel(c_ref, w_ref, b_ref, o_ref):
    c = c_ref[...]
    s = c * _sigmoid(c)
    o_ref[...] = jnp.dot(s, w_ref[...], preferred_element_type=F32) + b_ref[...]


def _mod_call(c_rows, w_mod, b_mod):
    n_layers, d, n = w_mod.shape
    tn = _pick_tile(n, TN_MOD)
    return pl.pallas_call(
        _mod_kernel,
        grid=(n_layers, n // tn),
        in_specs=[
            pl.BlockSpec((SUBLANES, d), lambda l, j: (0, 0)),
            pl.BlockSpec((None, d, tn), lambda l, j: (l, 0, j)),
            pl.BlockSpec((None, 1, tn), lambda l, j: (l, 0, j)),
        ],
        out_specs=pl.BlockSpec((None, SUBLANES, tn), lambda l, j: (l, 0, j)),
        out_shape=jax.ShapeDtypeStruct((n_layers, SUBLANES, n), F32),
        compiler_params=_cparams(("parallel", "parallel")),
        name="mod",
    )(c_rows, w_mod, b_mod.reshape(n_layers, 1, n))


def _ffn_kernel(x_ref, mod_ref, win_ref, wout_ref, o_ref, xn_ref, *, mod_base, tf):
    j = pl.program_id(1)

    @pl.when(j == 0)
    def _():
        xn = _modulated_norm(x_ref[...], mod_ref[mod_base:mod_base + 1, :],
                             mod_ref[mod_base + 1:mod_base + 2, :])
        xn_ref[...] = xn.astype(BF16)

    h = jnp.dot(xn_ref[...], win_ref[...], preferred_element_type=F32)
    a = h[:, :tf]
    g = (a * _sigmoid(a) * h[:, tf:]).astype(BF16)
    p = jnp.dot(g, wout_ref[...], preferred_element_type=F32)

    @pl.when(j == 0)
    def _():
        o_ref[...] = p

    @pl.when(j > 0)
    def _():
        o_ref[...] += p

    @pl.when(j == pl.num_programs(1) - 1)
    def _():
        gate = mod_ref[mod_base + 2:mod_base + 3, :]
        o_ref[...] = x_ref[...] + (0.5 * gate) * o_ref[...]


def _ffn_call(h, mods, w_in_r, w_out, layer, mod_base, n_rows, group_of):
    d = h.shape[1]
    n_f = w_in_r.shape[1]
    tf = w_in_r.shape[3] // 2
    tm = TM_F
    return pl.pallas_call(
        functools.partial(_ffn_kernel, mod_base=mod_base, tf=tf),
        grid=(n_rows // tm, n_f),
        in_specs=[
            pl.BlockSpec((tm, d), lambda i, j: (i, 0)),
            pl.BlockSpec((None, None, N_MOD, d), lambda i, j: (layer, group_of(i), 0, 0)),
            pl.BlockSpec((None, None, d, 2 * tf), lambda i, j: (layer, j, 0, 0)),
            pl.BlockSpec((None, tf, d), lambda i, j: (layer, j, 0)),
        ],
        out_specs=pl.BlockSpec((tm, d), lambda i, j: (i, 0)),
        out_shape=jax.ShapeDtypeStruct((n_rows, d), F32),
        scratch_shapes=[pltpu.VMEM((tm, d), BF16)],
        compiler_params=_cparams(("parallel", "arbitrary")),
        name="ffn",
    )(h, mods, w_in_r, w_out)


def _rope_b(x, tab_ref):
    return (x * tab_ref[0] + pltpu.roll(x, ROPE_B // 2, 1) * tab_ref[1]
            + pltpu.roll(x, LANES - ROPE_B // 2, 1) * tab_ref[2])


def _rope_c(x, tab_ref):
    return x * tab_ref[3] + pltpu.roll(x, DH_C // 2, 1) * tab_ref[4]


def _prep_kernel(x_ref, mod_ref, wa_ref, wukv_ref, wuq_ref, gv_ref, tab_ref,
                 xn_ref, qb_ref, kb_ref, vb_ref, qc_ref, kc_ref, vc_ref, *, kvl, ql):
    xn = _modulated_norm(x_ref[...], mod_ref[3:4, :], mod_ref[4:5, :]).astype(BF16)
    xn_ref[...] = xn
    p = jnp.dot(xn, wa_ref[...], preferred_element_type=F32)
    o = 0
    ckv = p[:, o:o + kvl]; o += kvl
    kr = p[:, o:o + LANES]; o += LANES
    kc = p[:, o:o + HKV_C * DH_C]; o += HKV_C * DH_C
    vc = p[:, o:o + HKV_C * DH_C]; o += HKV_C * DH_C
    cq = p[:, o:o + ql]; o += ql
    qc = p[:, o:o + HC * DH_C]

    g_kv = gv_ref[0:1, 0:kvl]
    g_ql = gv_ref[1:2, 0:ql]
    gk_nope = gv_ref[2:3, 0:LANES]
    gk_pe = gv_ref[2:3, LANES:2 * LANES]
    gq_nope = gv_ref[2:3, 2 * LANES:3 * LANES]
    gq_pe = gv_ref[2:3, 3 * LANES:4 * LANES]
    g_kc = gv_ref[3:4, 0:LANES]
    g_qc = gv_ref[3:4, LANES:2 * LANES]

    def rms_rows(v):
        return v * lax.rsqrt(jnp.mean(v * v, axis=-1, keepdims=True) + EPS)

    ckv_n = (rms_rows(ckv) * g_kv).astype(BF16)
    kv = jnp.dot(ckv_n, wukv_ref[...], preferred_element_type=F32)
    kr_rot = _rope_b(kr * gk_pe, tab_ref)
    ss_r = jnp.sum(kr * kr, axis=-1, keepdims=True)
    for h in range(HB):
        kn = kv[:, h * NOPE_B:(h + 1) * NOPE_B]
        r = lax.rsqrt((jnp.sum(kn * kn, axis=-1, keepdims=True) + ss_r) * (1.0 / QK_B) + EPS)
        kb_ref[h, :, 0:NOPE_B] = (kn * r * gk_nope).astype(BF16)
        kb_ref[h, :, NOPE_B:NOPE_B + LANES] = (kr_rot * r).astype(BF16)
        vb_ref[h] = kv[:, HB * NOPE_B + h * V_B:HB * NOPE_B + (h + 1) * V_B].astype(BF16)

    cq_n = (rms_rows(cq) * g_ql).astype(BF16)
    q = jnp.dot(cq_n, wuq_ref[...], preferred_element_type=F32)
    for h in range(HB):
        qn = q[:, h * 2 * LANES:h * 2 * LANES + NOPE_B]
        qp = q[:, h * 2 * LANES + NOPE_B:(h + 1) * 2 * LANES]
        ss = jnp.sum(qn * qn, axis=-1, keepdims=True) + jnp.sum(qp * qp, axis=-1, keepdims=True)
        r = lax.rsqrt(ss * (1.0 / QK_B) + EPS) * (QK_B ** -0.5)
        qb_ref[h, :, 0:NOPE_B] = (qn * r * gq_nope).astype(BF16)
        qb_ref[h, :, NOPE_B:NOPE_B + LANES] = (_rope_b(qp * gq_pe, tab_ref) * r).astype(BF16)

    for h in range(HKV_C):
        kh = rms_rows(kc[:, h * DH_C:(h + 1) * DH_C]) * g_kc
        kc_ref[h] = _rope_c(kh, tab_ref).astype(BF16)
        vc_ref[h] = vc[:, h * DH_C:(h + 1) * DH_C].astype(BF16)
    for h in range(HC):
        qh = rms_rows(qc[:, h * DH_C:(h + 1) * DH_C]) * g_qc
        qc_ref[h] = (_rope_c(qh, tab_ref) * (DH_C ** -0.5)).astype(BF16)


def _prep_call(h, mods, w_attn, w_ukv_r, w_uq_r, gvec, tabs, layer, geo):
    t_rows, d = h.shape
    tm = TM_A
    n_a = w_attn.shape[2]
    kvl, ql = w_ukv_r.shape[1], w_uq_r.shape[1]
    bsz, p_len = geo["bsz"], geo["p_len"]
    bp = lambda i: (geo["batch_of"](i), 0, geo["pos_of"](i), 0)
    hd = lambda nh, dd: pl.BlockSpec((None, nh, tm, dd), bp)
    sh = lambda nh, dd: jax.ShapeDtypeStruct((bsz, nh, p_len, dd), BF16)
    return pl.pallas_call(
        functools.partial(_prep_kernel, kvl=kvl, ql=ql),
        grid=(t_rows // tm,),
        in_specs=[
            pl.BlockSpec((tm, d), lambda i: (i, 0)),
            pl.BlockSpec((None, None, N_MOD, d), lambda i: (layer, geo["group_a"](i), 0, 0)),
            pl.BlockSpec((None, d, n_a), lambda i: (layer, 0, 0)),
            pl.BlockSpec((None, kvl, w_ukv_r.shape[2]), lambda i: (layer, 0, 0)),
            pl.BlockSpec((None, ql, w_uq_r.shape[2]), lambda i: (layer, 0, 0)),
            pl.BlockSpec((None, SUBLANES, gvec.shape[2]), lambda i: (layer, 0, 0)),
            pl.BlockSpec((5, tm, LANES), lambda i: (0, geo["pos_of"](i), 0)),
        ],
        out_specs=[
            pl.BlockSpec((tm, d), lambda i: (i, 0)),
            hd(HB, 2 * LANES), hd(HB, 2 * LANES), hd(HB, V_B),
            hd(HC, DH_C), hd(HKV_C, DH_C), hd(HKV_C, DH_C),
        ],
        out_shape=[
            jax.ShapeDtypeStruct((t_rows, d), BF16),
            sh(HB, 2 * LANES), sh(HB, 2 * LANES), sh(HB, V_B),
            sh(HC, DH_C), sh(HKV_C, DH_C), sh(HKV_C, DH_C),
        ],
        compiler_params=_cparams(("parallel",)),
        name="attn_prep",
    )(h, mods, w_attn, w_ukv_r, w_uq_r, gvec, tabs)


def _mm_kernel(a_ref, w_ref, o_ref):
    o_ref[...] = jnp.dot(a_ref[...], w_ref[...], preferred_element_type=F32)


def _mix_proj_call(xn, w_mix, layer, n_rows):
    d = xn.shape[1]
    n = w_mix.shape[2]
    tm, tn = TM_F, _pick_tile(n, TN_MIX)
    return pl.pallas_call(
        _mm_kernel,
        grid=(n_rows // tm, n // tn),
        in_specs=[
            pl.BlockSpec((tm, d), lambda i, j: (i, 0)),
            pl.BlockSpec((None, d, tn), lambda i, j: (layer, 0, j)),
        ],
        out_specs=pl.BlockSpec((tm, tn), lambda i, j: (i, j)),
        out_shape=jax.ShapeDtypeStruct((n_rows, n), F32),
        compiler_params=_cparams(("parallel", "parallel")),
        name="mix_proj",
    )(xn, w_mix)


def _attn_kernel(q_ref, k_ref, v_ref, o_ref, m_ref, l_ref, acc_ref, *,
                 group, tq, dv, lat_q_tiles, ctx_chunk0, n_chunks):
    qi = pl.program_id(2)
    rows = group * tq
    q = q_ref[...].reshape(rows, q_ref.shape[-1])
    m_ref[...] = jnp.full(m_ref.shape, -jnp.inf, F32)
    l_ref[...] = jnp.zeros(l_ref.shape, F32)
    acc_ref[...] = jnp.zeros(acc_ref.shape, F32)
    lo = jnp.where(qi >= lat_q_tiles, ctx_chunk0, 0)

    def body(c, carry):
        off = pl.multiple_of(c * TK, TK)
        k = k_ref[pl.ds(off, TK), :]
        v = v_ref[pl.ds(off, TK), :]
        s = lax.dot_general(q, k, (((1,), (1,)), ((), ())), preferred_element_type=F32)
        m_prev = m_ref[...]
        m_new = jnp.maximum(m_prev, jnp.max(s, axis=-1, keepdims=True))
        alpha = jnp.exp(m_prev - m_new)
        p = jnp.exp(s - pltpu.repeat(m_new, TK // LANES, 1))
        l_ref[...] = alpha * l_ref[...] + jnp.sum(p, axis=-1, keepdims=True)
        acc_ref[...] = alpha * acc_ref[...] + jnp.dot(p.astype(BF16), v, preferred_element_type=F32)
        m_ref[...] = m_new
        return carry

    lax.fori_loop(lo, n_chunks, body, 0)
    out = acc_ref[...] / l_ref[...]
    for g in range(group):
        o_ref[:, g * dv:(g + 1) * dv] = out[g * tq:(g + 1) * tq].astype(o_ref.dtype)


def _attn_call(q, k, v, group, geo, with_ctx_queries):
    bsz, n_heads, p_len, dk = q.shape
    n_kv = k.shape[1]
    dv = v.shape[-1]
    assert dv == LANES and n_heads == n_kv * group
    tq = TM_A
    lat_q_tiles = geo["seq"] // tq
    ctx_q_tiles = geo["ctx"] // tq
    q_tiles = lat_q_tiles + (ctx_q_tiles if with_ctx_queries else 0)
    n_rows = geo["t_rows"] if with_ctx_queries else geo["n_lat"]
    n_lat_blocks = geo["n_lat"] // tq

    def out_row(b, qi):
        return jnp.where(qi < lat_q_tiles, b * lat_q_tiles + qi,
                         n_lat_blocks + b * ctx_q_tiles + qi - lat_q_tiles)

    return pl.pallas_call(
        functools.partial(_attn_kernel, group=group, tq=tq, dv=dv, lat_q_tiles=lat_q_tiles,
                          ctx_chunk0=geo["seq"] // TK, n_chunks=p_len // TK),
        grid=(bsz, n_kv, q_tiles),
        in_specs=[
            pl.BlockSpec((None, group, tq, dk), lambda b, h, qi: (b, h, qi, 0)),
            pl.BlockSpec((None, None, p_len, dk), lambda b, h, qi: (b, h, 0, 0)),
            pl.BlockSpec((None, None, p_len, dv), lambda b, h, qi: (b, h, 0, 0)),
        ],
        out_specs=pl.BlockSpec((tq, group * dv), lambda b, h, qi: (out_row(b, qi), h)),
        out_shape=jax.ShapeDtypeStruct((n_rows, n_heads * dv), BF16),
        scratch_shapes=[
            pltpu.VMEM((group * tq, LANES), F32),
            pltpu.VMEM((group * tq, LANES), F32),
            pltpu.VMEM((group * tq, dv), F32),
        ],
        compiler_params=_cparams(("parallel", "parallel", "arbitrary")),
        name="attn_g%d" % group,
    )(q, k, v)


def _conv_a_kernel(cur_ref, prev_ref, next_ref, w_ref, vec_ref, o_ref, ext_ref, y_ref, *,
                   d_a, tm, first_of, last_of):
    i = pl.program_id(0)

    def glu(u):
        return u[:, :d_a] * _sigmoid(u[:, d_a:])

    keep_prev = jnp.where(first_of(i), 0.0, 1.0)
    keep_next = jnp.where(last_of(i), 0.0, 1.0)
    ext_ref[0, 0:HALO_A, :] = glu(prev_ref[...]) * keep_prev
    ext_ref[0, HALO_A:HALO_A + tm, :] = glu(cur_ref[...])
    ext_ref[0, HALO_A + tm:2 * HALO_A + tm, :] = glu(next_ref[...]) * keep_next
    span = tm + 2 * HALO_A - SUBLANES
    for r in range(1, SUBLANES):
        ext_ref[r, 0:span, :] = ext_ref[0, r:r + span, :]

    def lane_chunk(c, carry):
        cs = pl.multiple_of(c * CONV_LANES, CONV_LANES)
        for rc in range(tm // CONV_ROWS):
            acc = jnp.zeros((CONV_ROWS, CONV_LANES), F32)
            for tap in range(CONV_A):
                s = tap + HALO_A - CONV_A // 2
                q8, r = divmod(s, SUBLANES)
                row0 = q8 * SUBLANES + rc * CONV_ROWS
                acc = acc + (w_ref[tap:tap + 1, pl.ds(cs, CONV_LANES)]
                             * ext_ref[r, row0:row0 + CONV_ROWS, pl.ds(cs, CONV_LANES)])
            y_ref[rc * CONV_ROWS:(rc + 1) * CONV_ROWS, pl.ds(cs, CONV_LANES)] = acc
        return carry

    lax.fori_loop(0, d_a // CONV_LANES, lane_chunk, 0)
    y = y_ref[...] + vec_ref[0:1, :]
    mu = jnp.mean(y, axis=-1, keepdims=True)
    yc = y - mu
    var = jnp.mean(yc * yc, axis=-1, keepdims=True)
    z = yc * lax.rsqrt(var + EPS) * vec_ref[1:2, :] + vec_ref[2:3, :]
    o_ref[...] = (z * _sigmoid(z)).astype(o_ref.dtype)


def _conv_a_call(p2, w_dw, vecs, layer, n_rows, geo):
    d_a = w_dw.shape[2]
    tm = TM_A
    hb = tm // HALO_A
    n_halo_blocks = p2.shape[0] // HALO_A
    return pl.pallas_call(
        functools.partial(_conv_a_kernel, d_a=d_a, tm=tm,
                          first_of=geo["first_a"], last_of=geo["last_a"]),
        grid=(n_rows // tm,),
        in_specs=[
            pl.BlockSpec((tm, 2 * d_a), lambda i: (i, 0)),
            pl.BlockSpec((HALO_A, 2 * d_a), lambda i: (jnp.maximum(i * hb - 1, 0), 0)),
            pl.BlockSpec((HALO_A, 2 * d_a),
                         lambda i: (jnp.minimum((i + 1) * hb, n_halo_blocks - 1), 0)),
            pl.BlockSpec((None, CONV_A, d_a), lambda i: (layer, 0, 0)),
            pl.BlockSpec((None, SUBLANES, d_a), lambda i: (layer, 0, 0)),
        ],
        out_specs=pl.BlockSpec((tm, d_a), lambda i: (i, 0)),
        out_shape=jax.ShapeDtypeStruct((n_rows, d_a), BF16),
        scratch_shapes=[
            pltpu.VMEM((SUBLANES, tm + 2 * HALO_A, d_a), F32),
            pltpu.VMEM((tm, d_a), F32),
        ],
        compiler_params=_cparams(("parallel",)),
        name="conv_a",
    )(p2, p2, p2, w_dw, vecs)


def _conv_d_kernel(bg_ref, cg_ref, hh_ref, cg_prev_ref, hh_prev_ref, cg_next_ref, hh_next_ref,
                   w_ref, o_ref, ext_ref, *, tm, first_of, last_of):
    i = pl.program_id(0)
    keep_prev = jnp.where(first_of(i), 0.0, 1.0)
    keep_next = jnp.where(last_of(i), 0.0, 1.0)
    ext_ref[0:HALO_D, :] = cg_prev_ref[...] * hh_prev_ref[...] * keep_prev
    ext_ref[HALO_D:HALO_D + tm, :] = cg_ref[...] * hh_ref[...]
    ext_ref[HALO_D + tm:2 * HALO_D + tm, :] = cg_next_ref[...] * hh_next_ref[...] * keep_next
    acc = None
    for tap in range(CONV_D):
        s = tap + HALO_D - CONV_D // 2
        term = w_ref[tap:tap + 1, :] * ext_ref[s:s + tm, :]
        acc = term if acc is None else acc + term
    o_ref[...] = (bg_ref[...] * acc).astype(o_ref.dtype)


def _conv_d_call(p2, w_dw, layer, n_rows, col0, geo):
    d_d = w_dw.shape[2]
    tm = TM_A
    hb = tm // HALO_D
    n_halo_blocks = p2.shape[0] // HALO_D
    cb = col0 // d_d
    cur = lambda k: pl.BlockSpec((tm, d_d), lambda i: (i, cb + k))
    prev = lambda k: pl.BlockSpec((HALO_D, d_d), lambda i: (jnp.maximum(i * hb - 1, 0), cb + k))
    nxt = lambda k: pl.BlockSpec(
        (HALO_D, d_d), lambda i: (jnp.minimum((i + 1) * hb, n_halo_blocks - 1), cb + k))
    return pl.pallas_call(
        functools.partial(_conv_d_kernel, tm=tm, first_of=geo["first_a"], last_of=geo["last_a"]),
        grid=(n_rows // tm,),
        in_specs=[cur(0), cur(1), cur(2), prev(1), prev(2), nxt(1), nxt(2),
                  pl.BlockSpec((None, SUBLANES, d_d), lambda i: (layer, 0, 0))],
        out_specs=pl.BlockSpec((tm, d_d), lambda i: (i, 0)),
        out_shape=jax.ShapeDtypeStruct((n_rows, d_d), BF16),
        scratch_shapes=[pltpu.VMEM((tm + 2 * HALO_D, d_d), F32)],
        compiler_params=_cparams(("parallel",)),
        name="conv_d",
    )(p2, p2, p2, p2, p2, p2, p2, w_dw)


def _merge_kernel(h_ref, mod_ref, xn_ref, aa_ref, ab_ref, ac_ref, ad_ref,
                  wg_ref, wbr_ref, wo_ref, o_ref, *, tn):
    j = pl.program_id(1)
    gl = jnp.dot(xn_ref[...], wg_ref[...], preferred_element_type=F32)
    y = None
    for b, a_ref in enumerate((aa_ref, ab_ref, ac_ref, ad_ref)):
        yb = jnp.dot(a_ref[...], wbr_ref[b], preferred_element_type=F32)
        term = _sigmoid(gl[:, b * tn:(b + 1) * tn]) * yb
        y = term if y is None else y + term
    p = jnp.dot(y.astype(BF16), wo_ref[...], preferred_element_type=F32)

    @pl.when(j == 0)
    def _():
        o_ref[...] = p

    @pl.when(j > 0)
    def _():
        o_ref[...] += p

    @pl.when(j == pl.num_programs(1) - 1)
    def _():
        o_ref[...] = h_ref[...] + mod_ref[5:6, :] * o_ref[...]


def _merge_call(h, mods, xn, aa, ab, ac, ad, wg_r, wbr_r, w_o, layer, n_rows, group_of):
    d = h.shape[1]
    n_j = wg_r.shape[1]
    tn = wg_r.shape[3] // N_BRANCH
    d_br = wbr_r.shape[3]
    tm = TM_F
    act = pl.BlockSpec((tm, d_br), lambda i, j: (i, 0))
    return pl.pallas_call(
        functools.partial(_merge_kernel, tn=tn),
        grid=(n_rows // tm, n_j),
        in_specs=[
            pl.BlockSpec((tm, d), lambda i, j: (i, 0)),
            pl.BlockSpec((None, None, N_MOD, d), lambda i, j: (layer, group_of(i), 0, 0)),
            pl.BlockSpec((tm, d), lambda i, j: (i, 0)),
            act, act, act, act,
            pl.BlockSpec((None, None, d, N_BRANCH * tn), lambda i, j: (layer, j, 0, 0)),
            pl.BlockSpec((None, None, N_BRANCH, d_br, tn), lambda i, j: (layer, j, 0, 0, 0)),
            pl.BlockSpec((None, tn, d), lambda i, j: (layer, j, 0)),
        ],
        out_specs=pl.BlockSpec((tm, d), lambda i, j: (i, 0)),
        out_shape=jax.ShapeDtypeStruct((n_rows, d), F32),
        compiler_params=_cparams(("parallel", "arbitrary")),
        name="merge",
    )(h, mods, xn, aa, ab, ac, ad, wg_r, wbr_r, w_o)


def _rope_tables(seq, ctx):
    t = jnp.arange(seq)
    row = (t // GRID_W).astype(F32)
    col = (t % GRID_W).astype(F32)

    def cos_sin(d_rot):
        n_freq = d_rot // 4
        inv = ROPE_BASE ** (-jnp.arange(n_freq, dtype=F32) / n_freq)
        ang = jnp.concatenate([row[:, None] * inv, col[:, None] * inv], axis=-1)
        pad = ((0, ctx), (0, 0))
        return (jnp.pad(jnp.cos(ang), pad, constant_values=1.0), jnp.pad(jnp.sin(ang), pad))

    cb, sb = cos_sin(ROPE_B)
    cc, sc = cos_sin(DH_C)
    zb = jnp.zeros_like(sb)
    fill = jnp.zeros((seq + ctx, LANES - ROPE_B), F32)
    return jnp.stack([
        jnp.concatenate([cb, cb, fill], axis=-1),
        jnp.concatenate([zb, sb, fill], axis=-1),
        jnp.concatenate([-sb, zb, fill], axis=-1),
        jnp.concatenate([cc, cc], axis=-1),
        jnp.concatenate([-sc, sc], axis=-1),
    ])


def kernel(x, c, ctx, c_ctx, w_mod, b_mod, w_ffn1_in, w_ffn1_out, w_ffn2_in, w_ffn2_out,
           w_in, g_q_lora, w_uq, g_kv_lora, w_ukv, g_q_b, g_k_b, w_o_b, g_q_c, g_k_c, w_o_c,
           w_dw_a, b_dw_a, g_ln_a, b_ln_a, w_out_a, w_dw_d, w_out_d, w_o):
    bsz, seq, d = x.shape
    ctx_len = ctx.shape[1]
    depth = w_mod.shape[0]
    d_ff = w_ffn1_out.shape[1]
    ql, kvl = g_q_lora.shape[1], g_kv_lora.shape[1]
    d_a, d_d = w_dw_a.shape[2], w_dw_d.shape[2]
    n_lat = bsz * seq
    t_rows = n_lat + bsz * ctx_len
    p_len = seq + ctx_len
    assert seq % TM_F == 0 and (bsz * ctx_len) % TM_F == 0 and ctx_len % TM_A == 0
    assert seq % TK == 0 and ctx_len % TK == 0 and bsz + 1 <= SUBLANES
    assert d_a == d_d == HB * V_B == HC * DH_C

    tps_a, tps_f = seq // TM_A, seq // TM_F
    nlt_a = bsz * tps_a
    ncb_a = ctx_len // TM_A
    geo = dict(
        bsz=bsz, seq=seq, ctx=ctx_len, p_len=p_len, n_lat=n_lat, t_rows=t_rows,
        group_a=lambda i: jnp.minimum(i // tps_a, bsz),
        batch_of=lambda i: jnp.where(i < nlt_a, i // tps_a, (i - nlt_a) // ncb_a),
        pos_of=lambda i: jnp.where(i < nlt_a, i % tps_a, tps_a + (i - nlt_a) % ncb_a),
        first_a=lambda i: jnp.where(i < nlt_a, i % tps_a == 0, (i - nlt_a) % ncb_a == 0),
        last_a=lambda i: jnp.where(i < nlt_a, i % tps_a == tps_a - 1,
                                   (i - nlt_a) % ncb_a == ncb_a - 1),
    )
    group_f = lambda i: jnp.minimum(i // tps_f, bsz)

    c_rows = jnp.concatenate(
        [c, c_ctx[None, :], jnp.zeros((SUBLANES - bsz - 1, d), F32)], axis=0)
    mods = _mod_call(c_rows, w_mod, b_mod).reshape(depth, SUBLANES, N_MOD, d)

    tf = _pick_tile(d_ff, TF)
    n_f = d_ff // tf

    def ffn_in_layout(w):
        w = w.astype(BF16).reshape(depth, d, 2, n_f, tf)
        return jnp.transpose(w, (0, 3, 1, 2, 4)).reshape(depth, n_f, d, 2 * tf)

    w1_in, w2_in = ffn_in_layout(w_ffn1_in), ffn_in_layout(w_ffn2_in)
    w1_out, w2_out = w_ffn1_out.astype(BF16), w_ffn2_out.astype(BF16)

    kv_cols = kvl + ROPE_B + 2 * HKV_C * DH_C
    attn_cols = kv_cols + ql + HC * DH_C
    mix_cols = 2 * d_a + 3 * d_d
    w_in_b = w_in.astype(BF16)
    w_attn = jnp.concatenate(
        [w_in_b[:, :, :kvl + ROPE_B], jnp.zeros((depth, d, LANES - ROPE_B), BF16),
         w_in_b[:, :, kvl + ROPE_B:attn_cols]], axis=-1)
    w_mix = w_in_b[:, :, attn_cols:attn_cols + mix_cols]
    tn = _pick_tile(d, TN_MERGE)
    n_j = d // tn
    wg_r = w_in_b[:, :, attn_cols + mix_cols:].reshape(depth, d, N_BRANCH, n_j, tn)
    wg_r = jnp.transpose(wg_r, (0, 3, 1, 2, 4)).reshape(depth, n_j, d, N_BRANCH * tn)
    wbr = jnp.stack([w_out_a, w_o_b, w_o_c, w_out_d], axis=1).astype(BF16)
    wbr_r = jnp.transpose(wbr.reshape(depth, N_BRANCH, d_a, n_j, tn), (0, 3, 1, 2, 4))
    w_o_bf = w_o.astype(BF16)

    w_ukv_r = w_ukv.astype(BF16).reshape(depth, kvl, HB, 2, NOPE_B)
    w_ukv_r = jnp.transpose(w_ukv_r, (0, 1, 3, 2, 4)).reshape(depth, kvl, 2 * HB * NOPE_B)
    w_uq_r = jnp.pad(w_uq.astype(BF16).reshape(depth, ql, HB, QK_B),
                     ((0, 0), (0, 0), (0, 0), (0, 2 * LANES - QK_B)))
    w_uq_r = w_uq_r.reshape(depth, ql, HB * 2 * LANES)

    gw = max(kvl, ql, 4 * LANES)
    padw = lambda v: jnp.pad(v, ((0, 0), (0, gw - v.shape[1])))
    pad_pe = lambda g: jnp.pad(g[:, NOPE_B:], ((0, 0), (0, LANES - ROPE_B)))
    gvec = jnp.stack([
        padw(g_kv_lora), padw(g_q_lora),
        padw(jnp.concatenate([g_k_b[:, :NOPE_B], pad_pe(g_k_b),
                              g_q_b[:, :NOPE_B], pad_pe(g_q_b)], axis=-1)),
        padw(jnp.concatenate([g_k_c, g_q_c], axis=-1)),
    ] + [jnp.zeros((depth, gw), F32)] * (SUBLANES - 4), axis=1)
    vecs_a = jnp.stack([b_dw_a, g_ln_a, b_ln_a]
                       + [jnp.zeros_like(b_dw_a)] * (SUBLANES - 3), axis=1)
    w_dw_d_p = jnp.pad(w_dw_d, ((0, 0), (0, SUBLANES - CONV_D), (0, 0)))
    tabs = _rope_tables(seq, ctx_len)

    h = jnp.concatenate([x.reshape(n_lat, d), ctx.reshape(bsz * ctx_len, d)], axis=0)
    for layer in range(depth):
        last = layer == depth - 1
        rows_out = n_lat if last else t_rows
        h = _ffn_call(h, mods, w1_in, w1_out, layer, 0, t_rows, group_f)
        xn, qb, kb, vb, qc, kc, vc = _prep_call(
            h, mods, w_attn, w_ukv_r, w_uq_r, gvec, tabs, layer, geo)
        p2 = _mix_proj_call(xn, w_mix, layer, t_rows)
        ab = _attn_call(qb, kb, vb, 1, geo, not last)
        ac = _attn_call(qc, kc, vc, GROUP_C, geo, not last)
        aa = _conv_a_call(p2, w_dw_a, vecs_a, layer, rows_out, geo)
        ad = _conv_d_call(p2, w_dw_d_p, layer, rows_out, 2 * d_a, geo)
        h = _merge_call(h, mods, xn, aa, ab, ac, ad, wg_r, wbr_r, w_o_bf, layer, rows_out, group_f)
        h = _ffn_call(h, mods, w2_in, w2_out, layer, 6, rows_out, group_f)
    return h.reshape(bsz, seq, d)
```

```python
import functools

import jax
import jax.numpy as jnp
from jax import lax
from jax.experimental import pallas as pl
from jax.experimental.pallas import tpu as pltpu

F32 = jnp.float32
BF16 = jnp.bfloat16

GRID_W = 64
N_MOD = 9
HB = 8
NOPE_B = 128
ROPE_B = 64
V_B = 128
QK_B = NOPE_B + ROPE_B
HC = 8
HKV_C = 2
DH_C = 128
GROUP_C = HC // HKV_C
CONV_A = 31
CONV_D = 3
N_BRANCH = 4
ROPE_BASE = 10000.0
EPS = 1e-6

LANES = 128
SUBLANES = 8
VMEM_LIMIT = 56 * 1024 * 1024

TM_A = 256
TM_F = 512
TF = 512
TN_MERGE = 256
TN_MOD = 1024
TN_MIX = 512
TM_MIX = 1024
TK = 256
TQ_ROWS = 1024
ATTN_ROW_BLOCK = 128
LOG2_E = 1.4426950408889634
HALO_A = 16
HALO_D = 8
CONV_ROWS = 64
CONV_LANES = 128


def _cparams(sem):
    return pltpu.CompilerParams(dimension_semantics=sem, vmem_limit_bytes=VMEM_LIMIT)


def _pick_tile(n, cap):
    t = min(cap, n) // LANES * LANES
    while n % t:
        t -= LANES
    return t


def _sigmoid(x):
    return 1.0 / (1.0 + jnp.exp(-x))


def _modulated_norm(x, shift, scale):
    y = x * lax.rsqrt(jnp.mean(x * x, axis=-1, keepdims=True) + EPS)
    return y * (1.0 + scale) + shift


def _mod_kernel(c_ref, w_ref, b_ref, o_ref):
    c = c_ref[...]
    s = c * _sigmoid(c)
    o_ref[...] = jnp.dot(s, w_ref[...], preferred_element_type=F32) + b_ref[...]


def _mod_call(c_rows, w_mod, b_mod):
    n_layers, d, n = w_mod.shape
    tn = _pick_tile(n, TN_MOD)
    return pl.pallas_call(
        _mod_kernel,
        grid=(n_layers, n // tn),
        in_specs=[
            pl.BlockSpec((SUBLANES, d), lambda l, j: (0, 0)),
            pl.BlockSpec((None, d, tn), lambda l, j: (l, 0, j)),
            pl.BlockSpec((None, 1, tn), lambda l, j: (l, 0, j)),
        ],
        out_specs=pl.BlockSpec((None, SUBLANES, tn), lambda l, j: (l, 0, j)),
        out_shape=jax.ShapeDtypeStruct((n_layers, SUBLANES, n), F32),
        compiler_params=_cparams(("parallel", "parallel")),
        name="mod",
    )(c_rows, w_mod, b_mod.reshape(n_layers, 1, n))


def _ffn_kernel(x_ref, mod_ref, wa_ref, wb_ref, wout_ref, o_ref, xn_ref, *, mod_base):
    j = pl.program_id(1)

    @pl.when(j == 0)
    def _():
        xn = _modulated_norm(x_ref[...], mod_ref[mod_base:mod_base + 1, :],
                             mod_ref[mod_base + 1:mod_base + 2, :])
        xn_ref[...] = xn.astype(BF16)

    xn = xn_ref[...]
    a = jnp.dot(xn, wa_ref[...], preferred_element_type=F32)
    b = jnp.dot(xn, wb_ref[...], preferred_element_type=F32)
    g = (a * _sigmoid(a) * b).astype(BF16)
    p = jnp.dot(g, wout_ref[...], preferred_element_type=F32)

    @pl.when(j == 0)
    def _():
        o_ref[...] = p

    @pl.when(j > 0)
    def _():
        o_ref[...] += p

    @pl.when(j == pl.num_programs(1) - 1)
    def _():
        gate = mod_ref[mod_base + 2:mod_base + 3, :]
        o_ref[...] = x_ref[...] + (0.5 * gate) * o_ref[...]


def _ffn_call(h, mods, w_in, w_out, layer, mod_base, n_rows, group_of):
    d = h.shape[1]
    d_ff = w_out.shape[1]
    tf = _pick_tile(d_ff, TF)
    n_f = d_ff // tf
    tm = TM_F
    return pl.pallas_call(
        functools.partial(_ffn_kernel, mod_base=mod_base),
        grid=(n_rows // tm, n_f),
        in_specs=[
            pl.BlockSpec((tm, d), lambda i, j: (i, 0)),
            pl.BlockSpec((None, None, N_MOD, d), lambda i, j: (layer, group_of(i), 0, 0)),
            pl.BlockSpec((None, d, tf), lambda i, j: (layer, 0, j)),
            pl.BlockSpec((None, d, tf), lambda i, j: (layer, 0, n_f + j)),
            pl.BlockSpec((None, tf, d), lambda i, j: (layer, j, 0)),
        ],
        out_specs=pl.BlockSpec((tm, d), lambda i, j: (i, 0)),
        out_shape=jax.ShapeDtypeStruct((n_rows, d), F32),
        scratch_shapes=[pltpu.VMEM((tm, d), BF16)],
        compiler_params=_cparams(("parallel", "arbitrary")),
        name="ffn",
    )(h, mods, w_in, w_in, w_out)


def _rope_b(x, tab_ref):
    return (x * tab_ref[0] + pltpu.roll(x, ROPE_B // 2, 1) * tab_ref[1]
            + pltpu.roll(x, LANES - ROPE_B // 2, 1) * tab_ref[2])


def _rope_c(x, tab_ref):
    return x * tab_ref[3] + pltpu.roll(x, DH_C // 2, 1) * tab_ref[4]


def _prep_kernel(x_ref, mod_ref, wa_ref, wukv_ref, wuq_ref, gv_ref, tab_ref,
                 xn_ref, qb_ref, kb_ref, vb_ref, qc_ref, kc_ref, vc_ref, *, kvl, ql):
    xn = _modulated_norm(x_ref[...], mod_ref[3:4, :], mod_ref[4:5, :]).astype(BF16)
    xn_ref[...] = xn
    p = jnp.dot(xn, wa_ref[...], preferred_element_type=F32)
    o = 0
    ckv = p[:, o:o + kvl]; o += kvl
    kr = p[:, o:o + LANES]; o += LANES
    kc = p[:, o:o + HKV_C * DH_C]; o += HKV_C * DH_C
    vc = p[:, o:o + HKV_C * DH_C]; o += HKV_C * DH_C
    cq = p[:, o:o + ql]; o += ql
    qc = p[:, o:o + HC * DH_C]

    g_kv = gv_ref[0:1, 0:kvl]
    g_ql = gv_ref[1:2, 0:ql]
    gk_nope = gv_ref[2:3, 0:LANES]
    gk_pe = gv_ref[2:3, LANES:2 * LANES]
    gq_nope = gv_ref[2:3, 2 * LANES:3 * LANES]
    gq_pe = gv_ref[2:3, 3 * LANES:4 * LANES]
    g_kc = gv_ref[3:4, 0:LANES]
    g_qc = gv_ref[3:4, LANES:2 * LANES]

    def rms_rows(v):
        return v * lax.rsqrt(jnp.mean(v * v, axis=-1, keepdims=True) + EPS)

    ckv_n = (rms_rows(ckv) * g_kv).astype(BF16)
    kv = jnp.dot(ckv_n, wukv_ref[...], preferred_element_type=F32)
    kr_rot = _rope_b(kr * gk_pe, tab_ref)
    ss_r = jnp.sum(kr * kr, axis=-1, keepdims=True)
    for h in range(HB):
        kn = kv[:, h * NOPE_B:(h + 1) * NOPE_B]
        r = lax.rsqrt((jnp.sum(kn * kn, axis=-1, keepdims=True) + ss_r) * (1.0 / QK_B) + EPS)
        kb_ref[h, 0:NOPE_B, :] = (kn * r * gk_nope).T.astype(BF16)
        kb_ref[h, NOPE_B:NOPE_B + LANES, :] = (kr_rot * r).T.astype(BF16)
        vb_ref[h] = kv[:, HB * NOPE_B + h * V_B:HB * NOPE_B + (h + 1) * V_B].astype(BF16)

    cq_n = (rms_rows(cq) * g_ql).astype(BF16)
    q = jnp.dot(cq_n, wuq_ref[...], preferred_element_type=F32)
    for h in range(HB):
        qn = q[:, h * 2 * LANES:h * 2 * LANES + NOPE_B]
        qp = q[:, h * 2 * LANES + NOPE_B:(h + 1) * 2 * LANES]
        ss = jnp.sum(qn * qn, axis=-1, keepdims=True) + jnp.sum(qp * qp, axis=-1, keepdims=True)
        r = lax.rsqrt(ss * (1.0 / QK_B) + EPS) * (QK_B ** -0.5 * LOG2_E)
        qb_ref[h, :, 0:NOPE_B] = (qn * r * gq_nope).astype(BF16)
        qb_ref[h, :, NOPE_B:NOPE_B + LANES] = (_rope_b(qp * gq_pe, tab_ref) * r).astype(BF16)

    for h in range(HKV_C):
        kh = rms_rows(kc[:, h * DH_C:(h + 1) * DH_C]) * g_kc
        kc_ref[h] = _rope_c(kh, tab_ref).T.astype(BF16)
        vc_ref[h] = vc[:, h * DH_C:(h + 1) * DH_C].astype(BF16)
    for h in range(HC):
        qh = rms_rows(qc[:, h * DH_C:(h + 1) * DH_C]) * g_qc
        qc_ref[h] = (_rope_c(qh, tab_ref) * (DH_C ** -0.5 * LOG2_E)).astype(BF16)


def _prep_call(h, mods, w_attn, w_ukv_r, w_uq_r, gvec, tabs, layer, geo):
    t_rows, d = h.shape
    tm = TM_A
    n_a = w_attn.shape[2]
    kvl, ql = w_ukv_r.shape[1], w_uq_r.shape[1]
    bsz, p_len = geo["bsz"], geo["p_len"]
    bp = lambda i: (geo["batch_of"](i), 0, geo["pos_of"](i), 0)
    hd = lambda nh, dd: pl.BlockSpec((None, nh, tm, dd), bp)
    sh = lambda nh, dd: jax.ShapeDtypeStruct((bsz, nh, p_len, dd), BF16)
    bp_t = lambda i: (geo["batch_of"](i), 0, 0, geo["pos_of"](i))
    hd_t = lambda nh, dd: pl.BlockSpec((None, nh, dd, tm), bp_t)
    sh_t = lambda nh, dd: jax.ShapeDtypeStruct((bsz, nh, dd, p_len), BF16)
    return pl.pallas_call(
        functools.partial(_prep_kernel, kvl=kvl, ql=ql),
        grid=(t_rows // tm,),
        in_specs=[
            pl.BlockSpec((tm, d), lambda i: (i, 0)),
            pl.BlockSpec((None, None, N_MOD, d), lambda i: (layer, geo["group_a"](i), 0, 0)),
            pl.BlockSpec((None, d, n_a), lambda i: (layer, 0, 0)),
            pl.BlockSpec((None, kvl, w_ukv_r.shape[2]), lambda i: (layer, 0, 0)),
            pl.BlockSpec((None, ql, w_uq_r.shape[2]), lambda i: (layer, 0, 0)),
            pl.BlockSpec((None, SUBLANES, gvec.shape[2]), lambda i: (layer, 0, 0)),
            pl.BlockSpec((5, tm, LANES), lambda i: (0, geo["pos_of"](i), 0)),
        ],
        out_specs=[
            pl.BlockSpec((tm, d), lambda i: (i, 0)),
            hd(HB, 2 * LANES), hd_t(HB, 2 * LANES), hd(HB, V_B),
            hd(HC, DH_C), hd_t(HKV_C, DH_C), hd(HKV_C, DH_C),
        ],
        out_shape=[
            jax.ShapeDtypeStruct((t_rows, d), BF16),
            sh(HB, 2 * LANES), sh_t(HB, 2 * LANES), sh(HB, V_B),
            sh(HC, DH_C), sh_t(HKV_C, DH_C), sh(HKV_C, DH_C),
        ],
        compiler_params=_cparams(("parallel",)),
        name="attn_prep",
    )(h, mods, w_attn, w_ukv_r, w_uq_r, gvec, tabs)


def _mm_kernel(a_ref, w_ref, o_ref):
    o_ref[...] = jnp.dot(a_ref[...], w_ref[...], preferred_element_type=F32)


def _mix_proj_call(xn, w_mix, layer, n_rows):
    d = xn.shape[1]
    n = w_mix.shape[2]
    tm, tn = min(TM_MIX, n_rows), _pick_tile(n, TN_MIX)
    return pl.pallas_call(
        _mm_kernel,
        grid=(pl.cdiv(n_rows, tm), n // tn),
        in_specs=[
            pl.BlockSpec((tm, d), lambda i, j: (i, 0)),
            pl.BlockSpec((None, d, tn), lambda i, j: (layer, 0, j)),
        ],
        out_specs=pl.BlockSpec((tm, tn), lambda i, j: (i, j)),
        out_shape=jax.ShapeDtypeStruct((n_rows, n), F32),
        compiler_params=_cparams(("parallel", "parallel")),
        name="mix_proj",
    )(xn, w_mix)


def _attn_kernel(q_ref, k_ref, v_ref, o_ref, s0_ref, s1_ref, p_ref, alpha_ref, m_ref, l_ref,
                 acc_ref, *,
                 group, tq, dv, chunk_lo, chunk_hi):
    rows = group * tq
    q = q_ref[...].reshape(rows, q_ref.shape[-1])
    m_ref[...] = jnp.full(m_ref.shape, -jnp.inf, F32)
    l_ref[...] = jnp.zeros(l_ref.shape, F32)
    acc_ref[...] = jnp.zeros(acc_ref.shape, F32)

    def scores(c, s_ref):
        off = pl.multiple_of(c * TK, TK)
        s_ref[...] = jnp.dot(q, k_ref[:, pl.ds(off, TK)], preferred_element_type=F32)

    def update(c, s_ref):
        off = pl.multiple_of(c * TK, TK)
        for rb in range(rows // ATTN_ROW_BLOCK):
            sl = slice(rb * ATTN_ROW_BLOCK, (rb + 1) * ATTN_ROW_BLOCK)
            parts = [s_ref[sl, i * LANES:(i + 1) * LANES] for i in range(TK // LANES)]
            m_prev = m_ref[sl, :]
            m_cur = functools.reduce(jnp.maximum, parts)
            m_new = jnp.maximum(m_prev, jnp.max(m_cur, axis=-1, keepdims=True))
            alpha = jnp.exp2(m_prev - m_new)
            p_parts = [jnp.exp2(part - m_new) for part in parts]
            l_ref[sl, :] = alpha * l_ref[sl, :] + functools.reduce(jnp.add, p_parts)
            for i, pp in enumerate(p_parts):
                p_ref[sl, i * LANES:(i + 1) * LANES] = pp.astype(BF16)
            alpha_ref[sl, :] = alpha
            m_ref[sl, :] = m_new
        pv = jnp.dot(p_ref[...], v_ref[pl.ds(off, TK), :], preferred_element_type=F32)
        acc_ref[...] = alpha_ref[...] * acc_ref[...] + pv

    n_chunks = chunk_hi - chunk_lo
    n_pairs = (n_chunks - 1) // 2
    scores(chunk_lo, s0_ref)

    def pair(i, carry):
        c = chunk_lo + 2 * i
        scores(c + 1, s1_ref)
        update(c, s0_ref)
        scores(c + 2, s0_ref)
        update(c + 1, s1_ref)
        return carry

    lax.fori_loop(0, n_pairs, pair, 0)
    if n_chunks - 2 * n_pairs == 2:
        scores(chunk_hi - 1, s1_ref)
        update(chunk_hi - 2, s0_ref)
        update(chunk_hi - 1, s1_ref)
    else:
        update(chunk_hi - 1, s0_ref)
    out = acc_ref[...] / jnp.sum(l_ref[...], axis=-1, keepdims=True)
    for g in range(group):
        o_ref[:, g * dv:(g + 1) * dv] = out[g * tq:(g + 1) * tq].astype(o_ref.dtype)


def _attn_call(q, k, v, group, tq, q_pos0, q_len, key_lo, key_hi):
    bsz, n_heads, p_len, dk = q.shape
    n_kv = k.shape[1]
    dv = v.shape[-1]
    assert dv == LANES and n_heads == n_kv * group
    assert q_pos0 % tq == 0 and q_len % tq == 0 and key_lo % TK == 0 and key_hi % TK == 0
    q_tiles = q_len // tq
    q_blk0 = q_pos0 // tq
    return pl.pallas_call(
        functools.partial(_attn_kernel, group=group, tq=tq, dv=dv,
                          chunk_lo=key_lo // TK, chunk_hi=key_hi // TK),
        grid=(bsz, n_kv, q_tiles),
        in_specs=[
            pl.BlockSpec((None, group, tq, dk), lambda b, h, qi: (b, h, q_blk0 + qi, 0)),
            pl.BlockSpec((None, None, dk, p_len), lambda b, h, qi: (b, h, 0, 0)),
            pl.BlockSpec((None, None, p_len, dv), lambda b, h, qi: (b, h, 0, 0)),
        ],
        out_specs=pl.BlockSpec((tq, group * dv), lambda b, h, qi: (b * q_tiles + qi, h)),
        out_shape=jax.ShapeDtypeStruct((bsz * q_len, n_heads * dv), BF16),
        scratch_shapes=[
            pltpu.VMEM((group * tq, TK), F32),
            pltpu.VMEM((group * tq, TK), F32),
            pltpu.VMEM((group * tq, TK), BF16),
            pltpu.VMEM((group * tq, LANES), F32),
            pltpu.VMEM((group * tq, LANES), F32),
            pltpu.VMEM((group * tq, LANES), F32),
            pltpu.VMEM((group * tq, dv), F32),
        ],
        compiler_params=_cparams(("parallel", "parallel", "arbitrary")),
        name="attn_g%d_q%d" % (group, tq),
    )(q, k, v)


def _attention(q, k, v, group, tq_lat, geo, with_ctx_queries):
    seq, ctx_len, p_len = geo["seq"], geo["ctx"], geo["p_len"]
    out = _attn_call(q, k, v, group, tq_lat, 0, seq, 0, p_len)
    if with_ctx_queries:
        out_ctx = _attn_call(q, k, v, group, TM_A, seq, ctx_len, seq, p_len)
        out = jnp.concatenate([out, out_ctx], axis=0)
    return out


def _conv_a_kernel(cur_ref, prev_ref, next_ref, w_ref, vec_ref, o_ref, ext_ref, y_ref, *,
                   d_a, tm, first_of, last_of):
    i = pl.program_id(0)

    def glu(u):
        return u[:, :d_a] * _sigmoid(u[:, d_a:])

    keep_prev = jnp.where(first_of(i), 0.0, 1.0)
    keep_next = jnp.where(last_of(i), 0.0, 1.0)
    ext_ref[0, 0:HALO_A, :] = glu(prev_ref[...]) * keep_prev
    ext_ref[0, HALO_A:HALO_A + tm, :] = glu(cur_ref[...])
    ext_ref[0, HALO_A + tm:2 * HALO_A + tm, :] = glu(next_ref[...]) * keep_next
    span = tm + 2 * HALO_A - SUBLANES
    for r in range(1, SUBLANES):
        ext_ref[r, 0:span, :] = ext_ref[0, r:r + span, :]

    def lane_chunk(c, carry):
        cs = pl.multiple_of(c * CONV_LANES, CONV_LANES)
        for rc in range(tm // CONV_ROWS):
            acc = jnp.zeros((CONV_ROWS, CONV_LANES), F32)
            for tap in range(CONV_A):
                s = tap + HALO_A - CONV_A // 2
                q8, r = divmod(s, SUBLANES)
                row0 = q8 * SUBLANES + rc * CONV_ROWS
                acc = acc + (w_ref[tap:tap + 1, pl.ds(cs, CONV_LANES)]
                             * ext_ref[r, row0:row0 + CONV_ROWS, pl.ds(cs, CONV_LANES)])
            y_ref[rc * CONV_ROWS:(rc + 1) * CONV_ROWS, pl.ds(cs, CONV_LANES)] = acc
        return carry

    lax.fori_loop(0, d_a // CONV_LANES, lane_chunk, 0)
    y = y_ref[...] + vec_ref[0:1, :]
    mu = jnp.mean(y, axis=-1, keepdims=True)
    yc = y - mu
    var = jnp.mean(yc * yc, axis=-1, keepdims=True)
    z = yc * lax.rsqrt(var + EPS) * vec_ref[1:2, :] + vec_ref[2:3, :]
    o_ref[...] = (z * _sigmoid(z)).astype(o_ref.dtype)


def _conv_a_call(p2, w_dw, vecs, layer, n_rows, geo):
    d_a = w_dw.shape[2]
    tm = TM_A
    hb = tm // HALO_A
    n_halo_blocks = p2.shape[0] // HALO_A
    return pl.pallas_call(
        functools.partial(_conv_a_kernel, d_a=d_a, tm=tm,
                          first_of=geo["first_a"], last_of=geo["last_a"]),
        grid=(n_rows // tm,),
        in_specs=[
            pl.BlockSpec((tm, 2 * d_a), lambda i: (i, 0)),
            pl.BlockSpec((HALO_A, 2 * d_a), lambda i: (jnp.maximum(i * hb - 1, 0), 0)),
            pl.BlockSpec((HALO_A, 2 * d_a),
                         lambda i: (jnp.minimum((i + 1) * hb, n_halo_blocks - 1), 0)),
            pl.BlockSpec((None, CONV_A, d_a), lambda i: (layer, 0, 0)),
            pl.BlockSpec((None, SUBLANES, d_a), lambda i: (layer, 0, 0)),
        ],
        out_specs=pl.BlockSpec((tm, d_a), lambda i: (i, 0)),
        out_shape=jax.ShapeDtypeStruct((n_rows, d_a), BF16),
        scratch_shapes=[
            pltpu.VMEM((SUBLANES, tm + 2 * HALO_A, d_a), F32),
            pltpu.VMEM((tm, d_a), F32),
        ],
        compiler_params=_cparams(("parallel",)),
        name="conv_a",
    )(p2, p2, p2, w_dw, vecs)


def _conv_d_kernel(bg_ref, cg_ref, hh_ref, cg_prev_ref, hh_prev_ref, cg_next_ref, hh_next_ref,
                   w_ref, o_ref, ext_ref, *, tm, first_of, last_of):
    i = pl.program_id(0)
    keep_prev = jnp.where(first_of(i), 0.0, 1.0)
    keep_next = jnp.where(last_of(i), 0.0, 1.0)
    ext_ref[0:HALO_D, :] = cg_prev_ref[...] * hh_prev_ref[...] * keep_prev
    ext_ref[HALO_D:HALO_D + tm, :] = cg_ref[...] * hh_ref[...]
    ext_ref[HALO_D + tm:2 * HALO_D + tm, :] = cg_next_ref[...] * hh_next_ref[...] * keep_next
    acc = None
    for tap in range(CONV_D):
        s = tap + HALO_D - CONV_D // 2
        term = w_ref[tap:tap + 1, :] * ext_ref[s:s + tm, :]
        acc = term if acc is None else acc + term
    o_ref[...] = (bg_ref[...] * acc).astype(o_ref.dtype)


def _conv_d_call(p2, w_dw, layer, n_rows, col0, geo):
    d_d = w_dw.shape[2]
    tm = TM_A
    hb = tm // HALO_D
    n_halo_blocks = p2.shape[0] // HALO_D
    cb = col0 // d_d
    cur = lambda k: pl.BlockSpec((tm, d_d), lambda i: (i, cb + k))
    prev = lambda k: pl.BlockSpec((HALO_D, d_d), lambda i: (jnp.maximum(i * hb - 1, 0), cb + k))
    nxt = lambda k: pl.BlockSpec(
        (HALO_D, d_d), lambda i: (jnp.minimum((i + 1) * hb, n_halo_blocks - 1), cb + k))
    return pl.pallas_call(
        functools.partial(_conv_d_kernel, tm=tm, first_of=geo["first_a"], last_of=geo["last_a"]),
        grid=(n_rows // tm,),
        in_specs=[cur(0), cur(1), cur(2), prev(1), prev(2), nxt(1), nxt(2),
                  pl.BlockSpec((None, SUBLANES, d_d), lambda i: (layer, 0, 0))],
        out_specs=pl.BlockSpec((tm, d_d), lambda i: (i, 0)),
        out_shape=jax.ShapeDtypeStruct((n_rows, d_d), BF16),
        scratch_shapes=[pltpu.VMEM((tm + 2 * HALO_D, d_d), F32)],
        compiler_params=_cparams(("parallel",)),
        name="conv_d",
    )(p2, p2, p2, p2, p2, p2, p2, w_dw)


def _merge_kernel(h_ref, mod_ref, xn_ref, aa_ref, ab_ref, ac_ref, ad_ref,
                  wg0_ref, wg1_ref, wg2_ref, wg3_ref, wa_ref, wb_ref, wc_ref, wd_ref,
                  wo_ref, o_ref):
    j = pl.program_id(1)
    xn = xn_ref[...]
    y = None
    for a_ref, wg_ref, wbr_ref in ((aa_ref, wg0_ref, wa_ref), (ab_ref, wg1_ref, wb_ref),
                                   (ac_ref, wg2_ref, wc_ref), (ad_ref, wg3_ref, wd_ref)):
        gl = jnp.dot(xn, wg_ref[...], preferred_element_type=F32)
        yb = jnp.dot(a_ref[...], wbr_ref[...], preferred_element_type=F32)
        term = _sigmoid(gl) * yb
        y = term if y is None else y + term
    p = jnp.dot(y.astype(BF16), wo_ref[...], preferred_element_type=F32)

    @pl.when(j == 0)
    def _():
        o_ref[...] = p

    @pl.when(j > 0)
    def _():
        o_ref[...] += p

    @pl.when(j == pl.num_programs(1) - 1)
    def _():
        o_ref[...] = h_ref[...] + mod_ref[5:6, :] * o_ref[...]


def _merge_call(h, mods, xn, acts, w_gates, w_branches, w_o, layer, n_rows, group_of):
    d = h.shape[1]
    tn = _pick_tile(d, TN_MERGE)
    n_j = d // tn
    d_br = w_branches[0].shape[1]
    tm = TM_F
    act = pl.BlockSpec((tm, d_br), lambda i, j: (i, 0))
    gate = lambda b: pl.BlockSpec((None, d, tn), lambda i, j: (layer, 0, b * n_j + j))
    branch = pl.BlockSpec((None, d_br, tn), lambda i, j: (layer, 0, j))
    return pl.pallas_call(
        _merge_kernel,
        grid=(n_rows // tm, n_j),
        in_specs=[
            pl.BlockSpec((tm, d), lambda i, j: (i, 0)),
            pl.BlockSpec((None, None, N_MOD, d), lambda i, j: (layer, group_of(i), 0, 0)),
            pl.BlockSpec((tm, d), lambda i, j: (i, 0)),
            act, act, act, act,
            gate(0), gate(1), gate(2), gate(3),
            branch, branch, branch, branch,
            pl.BlockSpec((None, tn, d), lambda i, j: (layer, j, 0)),
        ],
        out_specs=pl.BlockSpec((tm, d), lambda i, j: (i, 0)),
        out_shape=jax.ShapeDtypeStruct((n_rows, d), F32),
        compiler_params=_cparams(("parallel", "arbitrary")),
        name="merge",
    )(h, mods, xn, *acts, w_gates, w_gates, w_gates, w_gates, *w_branches, w_o)


def _rope_tables(seq, ctx):
    t = jnp.arange(seq)
    row = (t // GRID_W).astype(F32)
    col = (t % GRID_W).astype(F32)

    def cos_sin(d_rot):
        n_freq = d_rot // 4
        inv = ROPE_BASE ** (-jnp.arange(n_freq, dtype=F32) / n_freq)
        ang = jnp.concatenate([row[:, None] * inv, col[:, None] * inv], axis=-1)
        pad = ((0, ctx), (0, 0))
        return (jnp.pad(jnp.cos(ang), pad, constant_values=1.0), jnp.pad(jnp.sin(ang), pad))

    cb, sb = cos_sin(ROPE_B)
    cc, sc = cos_sin(DH_C)
    zb = jnp.zeros_like(sb)
    fill = jnp.zeros((seq + ctx, LANES - ROPE_B), F32)
    return jnp.stack([
        jnp.concatenate([cb, cb, fill], axis=-1),
        jnp.concatenate([zb, sb, fill], axis=-1),
        jnp.concatenate([-sb, zb, fill], axis=-1),
        jnp.concatenate([cc, cc], axis=-1),
        jnp.concatenate([-sc, sc], axis=-1),
    ])


def kernel(x, c, ctx, c_ctx, w_mod, b_mod, w_ffn1_in, w_ffn1_out, w_ffn2_in, w_ffn2_out,
           w_in, g_q_lora, w_uq, g_kv_lora, w_ukv, g_q_b, g_k_b, w_o_b, g_q_c, g_k_c, w_o_c,
           w_dw_a, b_dw_a, g_ln_a, b_ln_a, w_out_a, w_dw_d, w_out_d, w_o):
    bsz, seq, d = x.shape
    ctx_len = ctx.shape[1]
    depth = w_mod.shape[0]
    d_ff = w_ffn1_out.shape[1]
    ql, kvl = g_q_lora.shape[1], g_kv_lora.shape[1]
    d_a, d_d = w_dw_a.shape[2], w_dw_d.shape[2]
    n_lat = bsz * seq
    t_rows = n_lat + bsz * ctx_len
    p_len = seq + ctx_len
    assert seq % TM_F == 0 and (bsz * ctx_len) % TM_F == 0 and ctx_len % TM_A == 0
    assert seq % TK == 0 and ctx_len % TK == 0 and bsz + 1 <= SUBLANES
    assert d_a == d_d == HB * V_B == HC * DH_C

    tps_a, tps_f = seq // TM_A, seq // TM_F
    nlt_a = bsz * tps_a
    ncb_a = ctx_len // TM_A
    geo = dict(
        bsz=bsz, seq=seq, ctx=ctx_len, p_len=p_len, n_lat=n_lat, t_rows=t_rows,
        group_a=lambda i: jnp.minimum(i // tps_a, bsz),
        batch_of=lambda i: jnp.where(i < nlt_a, i // tps_a, (i - nlt_a) // ncb_a),
        pos_of=lambda i: jnp.where(i < nlt_a, i % tps_a, tps_a + (i - nlt_a) % ncb_a),
        first_a=lambda i: jnp.where(i < nlt_a, i % tps_a == 0, (i - nlt_a) % ncb_a == 0),
        last_a=lambda i: jnp.where(i < nlt_a, i % tps_a == tps_a - 1,
                                   (i - nlt_a) % ncb_a == ncb_a - 1),
    )
    group_f = lambda i: jnp.minimum(i // tps_f, bsz)

    c_rows = jnp.concatenate(
        [c, c_ctx[None, :], jnp.zeros((SUBLANES - bsz - 1, d), F32)], axis=0)
    mods = _mod_call(c_rows, w_mod, b_mod).reshape(depth, SUBLANES, N_MOD, d)

    w1_in, w2_in = w_ffn1_in.astype(BF16), w_ffn2_in.astype(BF16)
    w1_out, w2_out = w_ffn1_out.astype(BF16), w_ffn2_out.astype(BF16)

    kv_cols = kvl + ROPE_B + 2 * HKV_C * DH_C
    attn_cols = kv_cols + ql + HC * DH_C
    mix_cols = 2 * d_a + 3 * d_d
    w_attn = jnp.concatenate(
        [w_in[:, :, :kvl + ROPE_B].astype(BF16), jnp.zeros((depth, d, LANES - ROPE_B), BF16),
         w_in[:, :, kvl + ROPE_B:attn_cols].astype(BF16)], axis=-1)
    w_mix = w_in[:, :, attn_cols:attn_cols + mix_cols].astype(BF16)
    w_gates = w_in[:, :, attn_cols + mix_cols:].astype(BF16)
    w_branches = [w.astype(BF16) for w in (w_out_a, w_o_b, w_o_c, w_out_d)]
    w_o_bf = w_o.astype(BF16)

    w_ukv_r = w_ukv.astype(BF16).reshape(depth, kvl, HB, 2, NOPE_B)
    w_ukv_r = jnp.transpose(w_ukv_r, (0, 1, 3, 2, 4)).reshape(depth, kvl, 2 * HB * NOPE_B)
    w_uq_r = jnp.pad(w_uq.astype(BF16).reshape(depth, ql, HB, QK_B),
                     ((0, 0), (0, 0), (0, 0), (0, 2 * LANES - QK_B)))
    w_uq_r = w_uq_r.reshape(depth, ql, HB * 2 * LANES)

    gw = max(kvl, ql, 4 * LANES)
    padw = lambda v: jnp.pad(v, ((0, 0), (0, gw - v.shape[1])))
    pad_pe = lambda g: jnp.pad(g[:, NOPE_B:], ((0, 0), (0, LANES - ROPE_B)))
    gvec = jnp.stack([
        padw(g_kv_lora), padw(g_q_lora),
        padw(jnp.concatenate([g_k_b[:, :NOPE_B], pad_pe(g_k_b),
                              g_q_b[:, :NOPE_B], pad_pe(g_q_b)], axis=-1)),
        padw(jnp.concatenate([g_k_c, g_q_c], axis=-1)),
    ] + [jnp.zeros((depth, gw), F32)] * (SUBLANES - 4), axis=1)
    vecs_a = jnp.stack([b_dw_a, g_ln_a, b_ln_a]
                       + [jnp.zeros_like(b_dw_a)] * (SUBLANES - 3), axis=1)
    w_dw_d_p = jnp.pad(w_dw_d, ((0, 0), (0, SUBLANES - CONV_D), (0, 0)))
    tabs = _rope_tables(seq, ctx_len)

    h = jnp.concatenate([x.reshape(n_lat, d), ctx.reshape(bsz * ctx_len, d)], axis=0)
    for layer in range(depth):
        last = layer == depth - 1
        rows_out = n_lat if last else t_rows
        h = _ffn_call(h, mods, w1_in, w1_out, layer, 0, t_rows, group_f)
        xn, qb, kb, vb, qc, kc, vc = _prep_call(
            h, mods, w_attn, w_ukv_r, w_uq_r, gvec, tabs, layer, geo)
        p2 = _mix_proj_call(xn, w_mix, layer, t_rows)
        ab = _attention(qb, kb, vb, 1, min(TQ_ROWS, seq), geo, not last)
        ac = _attention(qc, kc, vc, GROUP_C, min(TQ_ROWS // GROUP_C, seq), geo, not last)
        aa = _conv_a_call(p2, w_dw_a, vecs_a, layer, rows_out, geo)
        ad = _conv_d_call(p2, w_dw_d_p, layer, rows_out, 2 * d_a, geo)
        h = _merge_call(h, mods, xn, (aa, ab, ac, ad), w_gates, w_branches, w_o_bf,
                        layer, rows_out, group_f)
        h = _ffn_call(h, mods, w2_in, w2_out, layer, 6, rows_out, group_f)
    return h.reshape(bsz, seq, d)
```

```python
import functools

import jax
import jax.numpy as jnp
from jax import lax
from jax.experimental import pallas as pl
from jax.experimental.pallas import tpu as pltpu

F32 = jnp.float32
BF16 = jnp.bfloat16

GRID_W = 64
N_MOD = 9
HB = 8
NOPE_B = 128
ROPE_B = 64
V_B = 128
QK_B = NOPE_B + ROPE_B
HC = 8
HKV_C = 2
DH_C = 128
GROUP_C = HC // HKV_C
CONV_A = 31
CONV_D = 3
N_BRANCH = 4
ROPE_BASE = 10000.0
EPS = 1e-6

LANES = 128
SUBLANES = 8
VMEM_LIMIT = 56 * 1024 * 1024

TM_A = 256
TM_F = 512
TF = 512
TN_MERGE = 256
TN_MOD = 1024
TN_MIX = 512
TM_MIX = 1024
TK = 256
TQ_ROWS = 1024
ATTN_ROW_BLOCK = 128
ATTN_PAIRS_PER_TRIP = 4
LOG2_E = 1.4426950408889634
HALO_A = 16
HALO_D = 8
CONV_ROWS = 64
CONV_LANES = 128


def _cparams(sem):
    return pltpu.CompilerParams(dimension_semantics=sem, vmem_limit_bytes=VMEM_LIMIT)


def _pick_tile(n, cap):
    t = min(cap, n) // LANES * LANES
    while n % t:
        t -= LANES
    return t


def _sigmoid(x):
    return 1.0 / (1.0 + jnp.exp(-x))


def _modulated_norm(x, shift, scale):
    y = x * lax.rsqrt(jnp.mean(x * x, axis=-1, keepdims=True) + EPS)
    return y * (1.0 + scale) + shift


def _mod_kernel(c_ref, w_ref, b_ref, o_ref):
    c = c_ref[...]
    s = c * _sigmoid(c)
    o_ref[...] = jnp.dot(s, w_ref[...], preferred_element_type=F32) + b_ref[...]


def _mod_call(c_rows, w_mod, b_mod):
    n_layers, d, n = w_mod.shape
    tn = _pick_tile(n, TN_MOD)
    return pl.pallas_call(
        _mod_kernel,
        grid=(n_layers, n // tn),
        in_specs=[
            pl.BlockSpec((SUBLANES, d), lambda l, j: (0, 0)),
            pl.BlockSpec((None, d, tn), lambda l, j: (l, 0, j)),
            pl.BlockSpec((None, 1, tn), lambda l, j: (l, 0, j)),
        ],
        out_specs=pl.BlockSpec((None, SUBLANES, tn), lambda l, j: (l, 0, j)),
        out_shape=jax.ShapeDtypeStruct((n_layers, SUBLANES, n), F32),
        compiler_params=_cparams(("parallel", "parallel")),
        name="mod",
    )(c_rows, w_mod, b_mod.reshape(n_layers, 1, n))


def _ffn_kernel(x_ref, mod_ref, wa_ref, wb_ref, wout_ref, o_ref, xn_ref, *, mod_base):
    j = pl.program_id(1)

    @pl.when(j == 0)
    def _():
        xn = _modulated_norm(x_ref[...], mod_ref[mod_base:mod_base + 1, :],
                             mod_ref[mod_base + 1:mod_base + 2, :])
        xn_ref[...] = xn.astype(BF16)
        o_ref[...] = jnp.zeros(o_ref.shape, F32)

    xn = xn_ref[...]
    a = jnp.dot(xn, wa_ref[...], preferred_element_type=F32)
    b = jnp.dot(xn, wb_ref[...], preferred_element_type=F32)
    g = (a * _sigmoid(a) * b).astype(BF16)
    o_ref[...] += jnp.dot(g, wout_ref[...], preferred_element_type=F32)

    @pl.when(j == pl.num_programs(1) - 1)
    def _():
        gate = mod_ref[mod_base + 2:mod_base + 3, :]
        o_ref[...] = x_ref[...] + (0.5 * gate) * o_ref[...]


def _ffn_call(h, mods, w_in, w_out, layer, mod_base, n_rows, group_of):
    d = h.shape[1]
    d_ff = w_out.shape[1]
    tf = _pick_tile(d_ff, TF)
    n_f = d_ff // tf
    tm = TM_F
    return pl.pallas_call(
        functools.partial(_ffn_kernel, mod_base=mod_base),
        grid=(n_rows // tm, n_f),
        in_specs=[
            pl.BlockSpec((tm, d), lambda i, j: (i, 0)),
            pl.BlockSpec((None, None, N_MOD, d), lambda i, j: (layer, group_of(i), 0, 0)),
            pl.BlockSpec((None, d, tf), lambda i, j: (layer, 0, j)),
            pl.BlockSpec((None, d, tf), lambda i, j: (layer, 0, n_f + j)),
            pl.BlockSpec((None, tf, d), lambda i, j: (layer, j, 0)),
        ],
        out_specs=pl.BlockSpec((tm, d), lambda i, j: (i, 0)),
        out_shape=jax.ShapeDtypeStruct((n_rows, d), F32),
        scratch_shapes=[pltpu.VMEM((tm, d), BF16)],
        compiler_params=_cparams(("parallel", "arbitrary")),
        name="ffn",
    )(h, mods, w_in, w_in, w_out)


def _rope_b(x, tab_ref):
    return (x * tab_ref[0] + pltpu.roll(x, ROPE_B // 2, 1) * tab_ref[1]
            + pltpu.roll(x, LANES - ROPE_B // 2, 1) * tab_ref[2])


def _rope_c(x, tab_ref):
    return x * tab_ref[3] + pltpu.roll(x, DH_C // 2, 1) * tab_ref[4]


def _prep_kernel(x_ref, mod_ref, wa_ref, wukv_ref, wuq_ref, gv_ref, tab_ref,
                 xn_ref, qb_ref, kb_ref, vb_ref, qc_ref, kc_ref, vc_ref, *, kvl, ql):
    xn = _modulated_norm(x_ref[...], mod_ref[3:4, :], mod_ref[4:5, :]).astype(BF16)
    xn_ref[...] = xn
    p = jnp.dot(xn, wa_ref[...], preferred_element_type=F32)
    o = 0
    ckv = p[:, o:o + kvl]; o += kvl
    kr = p[:, o:o + LANES]; o += LANES
    kc = p[:, o:o + HKV_C * DH_C]; o += HKV_C * DH_C
    vc = p[:, o:o + HKV_C * DH_C]; o += HKV_C * DH_C
    cq = p[:, o:o + ql]; o += ql
    qc = p[:, o:o + HC * DH_C]

    g_kv = gv_ref[0:1, 0:kvl]
    g_ql = gv_ref[1:2, 0:ql]
    gk_nope = gv_ref[2:3, 0:LANES]
    gk_pe = gv_ref[2:3, LANES:2 * LANES]
    gq_nope = gv_ref[2:3, 2 * LANES:3 * LANES]
    gq_pe = gv_ref[2:3, 3 * LANES:4 * LANES]
    g_kc = gv_ref[3:4, 0:LANES]
    g_qc = gv_ref[3:4, LANES:2 * LANES]

    def rms_rows(v):
        return v * lax.rsqrt(jnp.mean(v * v, axis=-1, keepdims=True) + EPS)

    ckv_n = (rms_rows(ckv) * g_kv).astype(BF16)
    kv = jnp.dot(ckv_n, wukv_ref[...], preferred_element_type=F32)
    kr_rot = _rope_b(kr * gk_pe, tab_ref)
    ss_r = jnp.sum(kr * kr, axis=-1, keepdims=True)
    for h in range(HB):
        kn = kv[:, h * NOPE_B:(h + 1) * NOPE_B]
        r = lax.rsqrt((jnp.sum(kn * kn, axis=-1, keepdims=True) + ss_r) * (1.0 / QK_B) + EPS)
        kb_ref[h, 0:NOPE_B, :] = (kn * r * gk_nope).T.astype(BF16)
        kb_ref[h, NOPE_B:NOPE_B + LANES, :] = (kr_rot * r).T.astype(BF16)
        vb_ref[h] = kv[:, HB * NOPE_B + h * V_B:HB * NOPE_B + (h + 1) * V_B].astype(BF16)

    cq_n = (rms_rows(cq) * g_ql).astype(BF16)
    q = jnp.dot(cq_n, wuq_ref[...], preferred_element_type=F32)
    for h in range(HB):
        qn = q[:, h * 2 * LANES:h * 2 * LANES + NOPE_B]
        qp = q[:, h * 2 * LANES + NOPE_B:(h + 1) * 2 * LANES]
        ss = jnp.sum(qn * qn, axis=-1, keepdims=True) + jnp.sum(qp * qp, axis=-1, keepdims=True)
        r = lax.rsqrt(ss * (1.0 / QK_B) + EPS) * (QK_B ** -0.5 * LOG2_E)
        qb_ref[h, :, 0:NOPE_B] = (qn * r * gq_nope).astype(BF16)
        qb_ref[h, :, NOPE_B:NOPE_B + LANES] = (_rope_b(qp * gq_pe, tab_ref) * r).astype(BF16)

    for h in range(HKV_C):
        kh = rms_rows(kc[:, h * DH_C:(h + 1) * DH_C]) * g_kc
        kc_ref[h] = _rope_c(kh, tab_ref).T.astype(BF16)
        vc_ref[h] = vc[:, h * DH_C:(h + 1) * DH_C].astype(BF16)
    for h in range(HC):
        qh = rms_rows(qc[:, h * DH_C:(h + 1) * DH_C]) * g_qc
        qc_ref[h] = (_rope_c(qh, tab_ref) * (DH_C ** -0.5 * LOG2_E)).astype(BF16)


def _prep_call(h, mods, w_attn, w_ukv_r, w_uq_r, gvec, tabs, layer, geo):
    t_rows, d = h.shape
    tm = TM_A
    n_a = w_attn.shape[2]
    kvl, ql = w_ukv_r.shape[1], w_uq_r.shape[1]
    bsz, p_len = geo["bsz"], geo["p_len"]
    bp = lambda i: (geo["batch_of"](i), 0, geo["pos_of"](i), 0)
    hd = lambda nh, dd: pl.BlockSpec((None, nh, tm, dd), bp)
    sh = lambda nh, dd: jax.ShapeDtypeStruct((bsz, nh, p_len, dd), BF16)
    bp_t = lambda i: (geo["batch_of"](i), 0, 0, geo["pos_of"](i))
    hd_t = lambda nh, dd: pl.BlockSpec((None, nh, dd, tm), bp_t)
    sh_t = lambda nh, dd: jax.ShapeDtypeStruct((bsz, nh, dd, p_len), BF16)
    return pl.pallas_call(
        functools.partial(_prep_kernel, kvl=kvl, ql=ql),
        grid=(t_rows // tm,),
        in_specs=[
            pl.BlockSpec((tm, d), lambda i: (i, 0)),
            pl.BlockSpec((None, None, N_MOD, d), lambda i: (layer, geo["group_a"](i), 0, 0)),
            pl.BlockSpec((None, d, n_a), lambda i: (layer, 0, 0)),
            pl.BlockSpec((None, kvl, w_ukv_r.shape[2]), lambda i: (layer, 0, 0)),
            pl.BlockSpec((None, ql, w_uq_r.shape[2]), lambda i: (layer, 0, 0)),
            pl.BlockSpec((None, SUBLANES, gvec.shape[2]), lambda i: (layer, 0, 0)),
            pl.BlockSpec((5, tm, LANES), lambda i: (0, geo["pos_of"](i), 0)),
        ],
        out_specs=[
            pl.BlockSpec((tm, d), lambda i: (i, 0)),
            hd(HB, 2 * LANES), hd_t(HB, 2 * LANES), hd(HB, V_B),
            hd(HC, DH_C), hd_t(HKV_C, DH_C), hd(HKV_C, DH_C),
        ],
        out_shape=[
            jax.ShapeDtypeStruct((t_rows, d), BF16),
            sh(HB, 2 * LANES), sh_t(HB, 2 * LANES), sh(HB, V_B),
            sh(HC, DH_C), sh_t(HKV_C, DH_C), sh(HKV_C, DH_C),
        ],
        compiler_params=_cparams(("parallel",)),
        name="attn_prep",
    )(h, mods, w_attn, w_ukv_r, w_uq_r, gvec, tabs)


def _mm_kernel(a_ref, w_ref, o_ref):
    o_ref[...] = jnp.dot(a_ref[...], w_ref[...], preferred_element_type=F32)


def _mix_proj_call(xn, w_mix, layer, n_rows):
    d = xn.shape[1]
    n = w_mix.shape[2]
    tm, tn = min(TM_MIX, n_rows), _pick_tile(n, TN_MIX)
    return pl.pallas_call(
        _mm_kernel,
        grid=(pl.cdiv(n_rows, tm), n // tn),
        in_specs=[
            pl.BlockSpec((tm, d), lambda i, j: (i, 0)),
            pl.BlockSpec((None, d, tn), lambda i, j: (layer, 0, j)),
        ],
        out_specs=pl.BlockSpec((tm, tn), lambda i, j: (i, j)),
        out_shape=jax.ShapeDtypeStruct((n_rows, n), F32),
        compiler_params=_cparams(("parallel", "parallel")),
        name="mix_proj",
    )(xn, w_mix)


def _attn_kernel(q_ref, k_ref, v_ref, o_ref, s0_ref, s1_ref, p_ref, alpha_ref, m_ref, l_ref,
                 acc_ref, *,
                 group, tq, dv, key_lo, key_hi):
    rows = group * tq
    q = q_ref[...].reshape(rows, q_ref.shape[-1])
    m_ref[...] = jnp.full(m_ref.shape, -jnp.inf, F32)
    l_ref[...] = jnp.zeros(l_ref.shape, F32)
    acc_ref[...] = jnp.zeros(acc_ref.shape, F32)

    def scores(off, width, s_ref):
        s_ref[:, 0:width] = jnp.dot(q, k_ref[:, pl.ds(off, width)], preferred_element_type=F32)

    def update(off, width, s_ref):
        for rb in range(rows // ATTN_ROW_BLOCK):
            sl = slice(rb * ATTN_ROW_BLOCK, (rb + 1) * ATTN_ROW_BLOCK)
            parts = [s_ref[sl, i * LANES:(i + 1) * LANES] for i in range(width // LANES)]
            m_prev = m_ref[sl, :]
            m_cur = functools.reduce(jnp.maximum, parts)
            m_new = jnp.maximum(m_prev, jnp.max(m_cur, axis=-1, keepdims=True))
            alpha = jnp.exp2(m_prev - m_new)
            p_parts = [jnp.exp2(part - m_new) for part in parts]
            l_ref[sl, :] = alpha * l_ref[sl, :] + functools.reduce(jnp.add, p_parts)
            for i, pp in enumerate(p_parts):
                p_ref[sl, i * LANES:(i + 1) * LANES] = pp.astype(BF16)
            alpha_ref[sl, :] = alpha
            m_ref[sl, :] = m_new
        pv = jnp.dot(p_ref[:, 0:width], v_ref[pl.ds(off, width), :],
                     preferred_element_type=F32)
        acc_ref[...] = alpha_ref[...] * acc_ref[...] + pv

    n_full, tail = divmod(key_hi - key_lo, TK)
    n_chunks = n_full + (1 if tail else 0)
    bufs = (s0_ref, s1_ref)

    def chunk(idx):
        return key_lo + idx * TK, (TK if idx < n_full else tail)

    per_trip = 2 * ATTN_PAIRS_PER_TRIP
    n_trips = max(0, (n_full - 1) // per_trip)
    scores(*chunk(0), s0_ref)

    def trip(i, carry):
        off = pl.multiple_of(key_lo + per_trip * i * TK, TK)
        for k in range(per_trip):
            scores(off + (k + 1) * TK, TK, bufs[(k + 1) % 2])
            update(off + k * TK, TK, bufs[k % 2])
        return carry

    lax.fori_loop(0, n_trips, trip, 0)
    for idx in range(per_trip * n_trips, n_chunks):
        if idx + 1 < n_chunks:
            scores(*chunk(idx + 1), bufs[(idx + 1) % 2])
        update(*chunk(idx), bufs[idx % 2])
    out = acc_ref[...] / jnp.sum(l_ref[...], axis=-1, keepdims=True)
    for g in range(group):
        o_ref[:, g * dv:(g + 1) * dv] = out[g * tq:(g + 1) * tq].astype(o_ref.dtype)


def _attn_call(q, k, v, group, tq, q_pos0, q_len, key_lo, key_hi):
    bsz, n_heads, p_len, dk = q.shape
    n_kv = k.shape[1]
    dv = v.shape[-1]
    assert dv == LANES and n_heads == n_kv * group
    assert q_pos0 % tq == 0 and q_len % tq == 0
    assert key_lo % TK == 0 and (key_hi - key_lo) % LANES == 0
    q_tiles = q_len // tq
    q_blk0 = q_pos0 // tq
    return pl.pallas_call(
        functools.partial(_attn_kernel, group=group, tq=tq, dv=dv, key_lo=key_lo, key_hi=key_hi),
        grid=(bsz, n_kv, q_tiles),
        in_specs=[
            pl.BlockSpec((None, group, tq, dk), lambda b, h, qi: (b, h, q_blk0 + qi, 0)),
            pl.BlockSpec((None, None, dk, p_len), lambda b, h, qi: (b, h, 0, 0)),
            pl.BlockSpec((None, None, p_len, dv), lambda b, h, qi: (b, h, 0, 0)),
        ],
        out_specs=pl.BlockSpec((tq, group * dv), lambda b, h, qi: (b * q_tiles + qi, h)),
        out_shape=jax.ShapeDtypeStruct((bsz * q_len, n_heads * dv), BF16),
        scratch_shapes=[
            pltpu.VMEM((group * tq, TK), F32),
            pltpu.VMEM((group * tq, TK), F32),
            pltpu.VMEM((group * tq, TK), BF16),
            pltpu.VMEM((group * tq, LANES), F32),
            pltpu.VMEM((group * tq, LANES), F32),
            pltpu.VMEM((group * tq, LANES), F32),
            pltpu.VMEM((group * tq, dv), F32),
        ],
        compiler_params=_cparams(("parallel", "parallel", "arbitrary")),
        name="attn_g%d_q%d" % (group, tq),
    )(q, k, v)


def _attention(q, k, v, group, tq_lat, geo, with_ctx_queries):
    seq, ctx_len, p_len = geo["seq"], geo["ctx"], geo["p_len"]
    out = _attn_call(q, k, v, group, tq_lat, 0, seq, 0, p_len)
    if with_ctx_queries:
        out_ctx = _attn_call(q, k, v, group, TM_A, seq, ctx_len, seq, p_len)
        out = jnp.concatenate([out, out_ctx], axis=0)
    return out


def _conv_a_kernel(cur_ref, prev_ref, next_ref, w_ref, vec_ref, o_ref, ext_ref, y_ref, *,
                   d_a, tm, first_of, last_of):
    i = pl.program_id(0)

    def glu(u):
        return u[:, :d_a] * _sigmoid(u[:, d_a:])

    keep_prev = jnp.where(first_of(i), 0.0, 1.0)
    keep_next = jnp.where(last_of(i), 0.0, 1.0)
    ext_ref[0, 0:HALO_A, :] = glu(prev_ref[...]) * keep_prev
    ext_ref[0, HALO_A:HALO_A + tm, :] = glu(cur_ref[...])
    ext_ref[0, HALO_A + tm:2 * HALO_A + tm, :] = glu(next_ref[...]) * keep_next
    span = tm + 2 * HALO_A - SUBLANES
    for r in range(1, SUBLANES):
        ext_ref[r, 0:span, :] = ext_ref[0, r:r + span, :]

    def lane_chunk(c, carry):
        cs = pl.multiple_of(c * CONV_LANES, CONV_LANES)
        for rc in range(tm // CONV_ROWS):
            acc = jnp.zeros((CONV_ROWS, CONV_LANES), F32)
            for tap in range(CONV_A):
                s = tap + HALO_A - CONV_A // 2
                q8, r = divmod(s, SUBLANES)
                row0 = q8 * SUBLANES + rc * CONV_ROWS
                acc = acc + (w_ref[tap:tap + 1, pl.ds(cs, CONV_LANES)]
                             * ext_ref[r, row0:row0 + CONV_ROWS, pl.ds(cs, CONV_LANES)])
            y_ref[rc * CONV_ROWS:(rc + 1) * CONV_ROWS, pl.ds(cs, CONV_LANES)] = acc
        return carry

    lax.fori_loop(0, d_a // CONV_LANES, lane_chunk, 0)
    y = y_ref[...] + vec_ref[0:1, :]
    mu = jnp.mean(y, axis=-1, keepdims=True)
    yc = y - mu
    var = jnp.mean(yc * yc, axis=-1, keepdims=True)
    z = yc * lax.rsqrt(var + EPS) * vec_ref[1:2, :] + vec_ref[2:3, :]
    o_ref[...] = (z * _sigmoid(z)).astype(o_ref.dtype)


def _conv_a_call(p2, w_dw, vecs, layer, n_rows, geo):
    d_a = w_dw.shape[2]
    tm = TM_A
    hb = tm // HALO_A
    n_halo_blocks = p2.shape[0] // HALO_A
    return pl.pallas_call(
        functools.partial(_conv_a_kernel, d_a=d_a, tm=tm,
                          first_of=geo["first_a"], last_of=geo["last_a"]),
        grid=(n_rows // tm,),
        in_specs=[
            pl.BlockSpec((tm, 2 * d_a), lambda i: (i, 0)),
            pl.BlockSpec((HALO_A, 2 * d_a), lambda i: (jnp.maximum(i * hb - 1, 0), 0)),
            pl.BlockSpec((HALO_A, 2 * d_a),
                         lambda i: (jnp.minimum((i + 1) * hb, n_halo_blocks - 1), 0)),
            pl.BlockSpec((None, CONV_A, d_a), lambda i: (layer, 0, 0)),
            pl.BlockSpec((None, SUBLANES, d_a), lambda i: (layer, 0, 0)),
        ],
        out_specs=pl.BlockSpec((tm, d_a), lambda i: (i, 0)),
        out_shape=jax.ShapeDtypeStruct((n_rows, d_a), BF16),
        scratch_shapes=[
            pltpu.VMEM((SUBLANES, tm + 2 * HALO_A, d_a), F32),
            pltpu.VMEM((tm, d_a), F32),
        ],
        compiler_params=_cparams(("parallel",)),
        name="conv_a",
    )(p2, p2, p2, w_dw, vecs)


def _conv_d_kernel(bg_ref, cg_ref, hh_ref, cg_prev_ref, hh_prev_ref, cg_next_ref, hh_next_ref,
                   w_ref, o_ref, ext_ref, *, tm, first_of, last_of):
    i = pl.program_id(0)
    keep_prev = jnp.where(first_of(i), 0.0, 1.0)
    keep_next = jnp.where(last_of(i), 0.0, 1.0)
    ext_ref[0:HALO_D, :] = cg_prev_ref[...] * hh_prev_ref[...] * keep_prev
    ext_ref[HALO_D:HALO_D + tm, :] = cg_ref[...] * hh_ref[...]
    ext_ref[HALO_D + tm:2 * HALO_D + tm, :] = cg_next_ref[...] * hh_next_ref[...] * keep_next
    acc = None
    for tap in range(CONV_D):
        s = tap + HALO_D - CONV_D // 2
        term = w_ref[tap:tap + 1, :] * ext_ref[s:s + tm, :]
        acc = term if acc is None else acc + term
    o_ref[...] = (bg_ref[...] * acc).astype(o_ref.dtype)


def _conv_d_call(p2, w_dw, layer, n_rows, col0, geo):
    d_d = w_dw.shape[2]
    tm = TM_A
    hb = tm // HALO_D
    n_halo_blocks = p2.shape[0] // HALO_D
    cb = col0 // d_d
    cur = lambda k: pl.BlockSpec((tm, d_d), lambda i: (i, cb + k))
    prev = lambda k: pl.BlockSpec((HALO_D, d_d), lambda i: (jnp.maximum(i * hb - 1, 0), cb + k))
    nxt = lambda k: pl.BlockSpec(
        (HALO_D, d_d), lambda i: (jnp.minimum((i + 1) * hb, n_halo_blocks - 1), cb + k))
    return pl.pallas_call(
        functools.partial(_conv_d_kernel, tm=tm, first_of=geo["first_a"], last_of=geo["last_a"]),
        grid=(n_rows // tm,),
        in_specs=[cur(0), cur(1), cur(2), prev(1), prev(2), nxt(1), nxt(2),
                  pl.BlockSpec((None, SUBLANES, d_d), lambda i: (layer, 0, 0))],
        out_specs=pl.BlockSpec((tm, d_d), lambda i: (i, 0)),
        out_shape=jax.ShapeDtypeStruct((n_rows, d_d), BF16),
        scratch_shapes=[pltpu.VMEM((tm + 2 * HALO_D, d_d), F32)],
        compiler_params=_cparams(("parallel",)),
        name="conv_d",
    )(p2, p2, p2, p2, p2, p2, p2, w_dw)


def _merge_kernel(h_ref, mod_ref, xn_ref, aa_ref, ab_ref, ac_ref, ad_ref,
                  wg0_ref, wg1_ref, wg2_ref, wg3_ref, wa_ref, wb_ref, wc_ref, wd_ref,
                  wo_ref, o_ref):
    j = pl.program_id(1)

    @pl.when(j == 0)
    def _():
        o_ref[...] = jnp.zeros(o_ref.shape, F32)

    xn = xn_ref[...]
    y = None
    for a_ref, wg_ref, wbr_ref in ((aa_ref, wg0_ref, wa_ref), (ab_ref, wg1_ref, wb_ref),
                                   (ac_ref, wg2_ref, wc_ref), (ad_ref, wg3_ref, wd_ref)):
        gl = jnp.dot(xn, wg_ref[...], preferred_element_type=F32)
        yb = jnp.dot(a_ref[...], wbr_ref[...], preferred_element_type=F32)
        term = _sigmoid(gl) * yb
        y = term if y is None else y + term
    o_ref[...] += jnp.dot(y.astype(BF16), wo_ref[...], preferred_element_type=F32)

    @pl.when(j == pl.num_programs(1) - 1)
    def _():
        o_ref[...] = h_ref[...] + mod_ref[5:6, :] * o_ref[...]


def _merge_call(h, mods, xn, acts, w_gates, w_branches, w_o, layer, n_rows, group_of):
    d = h.shape[1]
    tn = _pick_tile(d, TN_MERGE)
    n_j = d // tn
    d_br = w_branches[0].shape[1]
    tm = TM_F
    act = pl.BlockSpec((tm, d_br), lambda i, j: (i, 0))
    gate = lambda b: pl.BlockSpec((None, d, tn), lambda i, j: (layer, 0, b * n_j + j))
    branch = pl.BlockSpec((None, d_br, tn), lambda i, j: (layer, 0, j))
    return pl.pallas_call(
        _merge_kernel,
        grid=(n_rows // tm, n_j),
        in_specs=[
            pl.BlockSpec((tm, d), lambda i, j: (i, 0)),
            pl.BlockSpec((None, None, N_MOD, d), lambda i, j: (layer, group_of(i), 0, 0)),
            pl.BlockSpec((tm, d), lambda i, j: (i, 0)),
            act, act, act, act,
            gate(0), gate(1), gate(2), gate(3),
            branch, branch, branch, branch,
            pl.BlockSpec((None, tn, d), lambda i, j: (layer, j, 0)),
        ],
        out_specs=pl.BlockSpec((tm, d), lambda i, j: (i, 0)),
        out_shape=jax.ShapeDtypeStruct((n_rows, d), F32),
        compiler_params=_cparams(("parallel", "arbitrary")),
        name="merge",
    )(h, mods, xn, *acts, w_gates, w_gates, w_gates, w_gates, *w_branches, w_o)


def _rope_tables(seq, ctx):
    t = jnp.arange(seq)
    row = (t // GRID_W).astype(F32)
    col = (t % GRID_W).astype(F32)

    def cos_sin(d_rot):
        n_freq = d_rot // 4
        inv = ROPE_BASE ** (-jnp.arange(n_freq, dtype=F32) / n_freq)
        ang = jnp.concatenate([row[:, None] * inv, col[:, None] * inv], axis=-1)
        pad = ((0, ctx), (0, 0))
        return (jnp.pad(jnp.cos(ang), pad, constant_values=1.0), jnp.pad(jnp.sin(ang), pad))

    cb, sb = cos_sin(ROPE_B)
    cc, sc = cos_sin(DH_C)
    zb = jnp.zeros_like(sb)
    fill = jnp.zeros((seq + ctx, LANES - ROPE_B), F32)
    return jnp.stack([
        jnp.concatenate([cb, cb, fill], axis=-1),
        jnp.concatenate([zb, sb, fill], axis=-1),
        jnp.concatenate([-sb, zb, fill], axis=-1),
        jnp.concatenate([cc, cc], axis=-1),
        jnp.concatenate([-sc, sc], axis=-1),
    ])


def kernel(x, c, ctx, c_ctx, w_mod, b_mod, w_ffn1_in, w_ffn1_out, w_ffn2_in, w_ffn2_out,
           w_in, g_q_lora, w_uq, g_kv_lora, w_ukv, g_q_b, g_k_b, w_o_b, g_q_c, g_k_c, w_o_c,
           w_dw_a, b_dw_a, g_ln_a, b_ln_a, w_out_a, w_dw_d, w_out_d, w_o):
    bsz, seq, d = x.shape
    ctx_len = ctx.shape[1]
    depth = w_mod.shape[0]
    d_ff = w_ffn1_out.shape[1]
    ql, kvl = g_q_lora.shape[1], g_kv_lora.shape[1]
    d_a, d_d = w_dw_a.shape[2], w_dw_d.shape[2]
    n_lat = bsz * seq
    t_rows = n_lat + bsz * ctx_len
    p_len = seq + ctx_len
    assert seq % TM_F == 0 and (bsz * ctx_len) % TM_F == 0 and ctx_len % TM_A == 0
    assert seq % TK == 0 and ctx_len % LANES == 0 and bsz + 1 <= SUBLANES
    assert d_a == d_d == HB * V_B == HC * DH_C

    tps_a, tps_f = seq // TM_A, seq // TM_F
    nlt_a = bsz * tps_a
    ncb_a = ctx_len // TM_A
    geo = dict(
        bsz=bsz, seq=seq, ctx=ctx_len, p_len=p_len, n_lat=n_lat, t_rows=t_rows,
        group_a=lambda i: jnp.minimum(i // tps_a, bsz),
        batch_of=lambda i: jnp.where(i < nlt_a, i // tps_a, (i - nlt_a) // ncb_a),
        pos_of=lambda i: jnp.where(i < nlt_a, i % tps_a, tps_a + (i - nlt_a) % ncb_a),
        first_a=lambda i: jnp.where(i < nlt_a, i % tps_a == 0, (i - nlt_a) % ncb_a == 0),
        last_a=lambda i: jnp.where(i < nlt_a, i % tps_a == tps_a - 1,
                                   (i - nlt_a) % ncb_a == ncb_a - 1),
    )
    group_f = lambda i: jnp.minimum(i // tps_f, bsz)

    c_rows = jnp.concatenate(
        [c, c_ctx[None, :], jnp.zeros((SUBLANES - bsz - 1, d), F32)], axis=0)
    mods = _mod_call(c_rows, w_mod, b_mod).reshape(depth, SUBLANES, N_MOD, d)

    w1_in, w2_in = w_ffn1_in.astype(BF16), w_ffn2_in.astype(BF16)
    w1_out, w2_out = w_ffn1_out.astype(BF16), w_ffn2_out.astype(BF16)

    kv_cols = kvl + ROPE_B + 2 * HKV_C * DH_C
    attn_cols = kv_cols + ql + HC * DH_C
    mix_cols = 2 * d_a + 3 * d_d
    w_attn = jnp.concatenate(
        [w_in[:, :, :kvl + ROPE_B].astype(BF16), jnp.zeros((depth, d, LANES - ROPE_B), BF16),
         w_in[:, :, kvl + ROPE_B:attn_cols].astype(BF16)], axis=-1)
    w_mix = w_in[:, :, attn_cols:attn_cols + mix_cols].astype(BF16)
    w_gates = w_in[:, :, attn_cols + mix_cols:].astype(BF16)
    w_branches = [w.astype(BF16) for w in (w_out_a, w_o_b, w_o_c, w_out_d)]
    w_o_bf = w_o.astype(BF16)

    w_ukv_r = w_ukv.astype(BF16).reshape(depth, kvl, HB, 2, NOPE_B)
    w_ukv_r = jnp.transpose(w_ukv_r, (0, 1, 3, 2, 4)).reshape(depth, kvl, 2 * HB * NOPE_B)
    w_uq_r = jnp.pad(w_uq.astype(BF16).reshape(depth, ql, HB, QK_B),
                     ((0, 0), (0, 0), (0, 0), (0, 2 * LANES - QK_B)))
    w_uq_r = w_uq_r.reshape(depth, ql, HB * 2 * LANES)

    gw = max(kvl, ql, 4 * LANES)
    padw = lambda v: jnp.pad(v, ((0, 0), (0, gw - v.shape[1])))
    pad_pe = lambda g: jnp.pad(g[:, NOPE_B:], ((0, 0), (0, LANES - ROPE_B)))
    gvec = jnp.stack([
        padw(g_kv_lora), padw(g_q_lora),
        padw(jnp.concatenate([g_k_b[:, :NOPE_B], pad_pe(g_k_b),
                              g_q_b[:, :NOPE_B], pad_pe(g_q_b)], axis=-1)),
        padw(jnp.concatenate([g_k_c, g_q_c], axis=-1)),
    ] + [jnp.zeros((depth, gw), F32)] * (SUBLANES - 4), axis=1)
    vecs_a = jnp.stack([b_dw_a, g_ln_a, b_ln_a]
                       + [jnp.zeros_like(b_dw_a)] * (SUBLANES - 3), axis=1)
    w_dw_d_p = jnp.pad(w_dw_d, ((0, 0), (0, SUBLANES - CONV_D), (0, 0)))
    tabs = _rope_tables(seq, ctx_len)

    h = jnp.concatenate([x.reshape(n_lat, d), ctx.reshape(bsz * ctx_len, d)], axis=0)
    for layer in range(depth):
        last = layer == depth - 1
        rows_out = n_lat if last else t_rows
        h = _ffn_call(h, mods, w1_in, w1_out, layer, 0, t_rows, group_f)
        xn, qb, kb, vb, qc, kc, vc = _prep_call(
            h, mods, w_attn, w_ukv_r, w_uq_r, gvec, tabs, layer, geo)
        p2 = _mix_proj_call(xn, w_mix, layer, t_rows)
        ab = _attention(qb, kb, vb, 1, min(TQ_ROWS, seq), geo, not last)
        ac = _attention(qc, kc, vc, GROUP_C, min(TQ_ROWS // GROUP_C, seq), geo, not last)
        aa = _conv_a_call(p2, w_dw_a, vecs_a, layer, rows_out, geo)
        ad = _conv_d_call(p2, w_dw_d_p, layer, rows_out, 2 * d_a, geo)
        h = _merge_call(h, mods, xn, (aa, ab, ac, ad), w_gates, w_branches, w_o_bf,
                        layer, rows_out, group_f)
        h = _ffn_call(h, mods, w2_in, w2_out, layer, 6, rows_out, group_f)
    return h.reshape(bsz, seq, d)
```

```python
import functools

import jax
import jax.numpy as jnp
from jax import lax
from jax.experimental import pallas as pl
from jax.experimental.pallas import tpu as pltpu

F32 = jnp.float32
BF16 = jnp.bfloat16

GRID_W = 64
N_MOD = 9
HB = 8
NOPE_B = 128
ROPE_B = 64
V_B = 128
QK_B = NOPE_B + ROPE_B
HC = 8
HKV_C = 2
DH_C = 128
GROUP_C = HC // HKV_C
CONV_A = 31
CONV_D = 3
N_BRANCH = 4
ROPE_BASE = 10000.0
EPS = 1e-6

LANES = 128
SUBLANES = 8
VMEM_LIMIT = 56 * 1024 * 1024

TM_A = 256
TM_F = 512
TM_FFN = 512
TF = 512
TN_MERGE = 256
TN_MOD = 1024
TN_MIX = 512
TM_MIX = 1024
TK = 256
TQ_ROWS = 1024
ATTN_ROW_BLOCK = 128
ATTN_PAIRS_PER_TRIP = 8
LOG2_E = 1.4426950408889634
HALO_A = 16
HALO_D = 8
CONV_ROWS = 64
CONV_LANES = 128


def _cparams(sem):
    return pltpu.CompilerParams(dimension_semantics=sem, vmem_limit_bytes=VMEM_LIMIT)


def _pick_tile(n, cap):
    t = min(cap, n) // LANES * LANES
    while n % t:
        t -= LANES
    return t


def _round_up(n, m):
    return -(-n // m) * m


def _sigmoid(x):
    return 1.0 / (1.0 + jnp.exp(-x))


def _modulated_norm(x, shift, scale):
    y = x * lax.rsqrt(jnp.mean(x * x, axis=-1, keepdims=True) + EPS)
    return y * (1.0 + scale) + shift


def _mod_kernel(c_ref, w_ref, b_ref, o_ref):
    c = c_ref[...]
    s = c * _sigmoid(c)
    o_ref[...] = jnp.dot(s, w_ref[...], preferred_element_type=F32) + b_ref[...]


def _mod_call(c_rows, w_mod, b_mod):
    n_layers, d, n = w_mod.shape
    tn = _pick_tile(n, TN_MOD)
    return pl.pallas_call(
        _mod_kernel,
        grid=(n_layers, n // tn),
        in_specs=[
            pl.BlockSpec((SUBLANES, d), lambda l, j: (0, 0)),
            pl.BlockSpec((None, d, tn), lambda l, j: (l, 0, j)),
            pl.BlockSpec((None, 1, tn), lambda l, j: (l, 0, j)),
        ],
        out_specs=pl.BlockSpec((None, SUBLANES, tn), lambda l, j: (l, 0, j)),
        out_shape=jax.ShapeDtypeStruct((n_layers, SUBLANES, n), F32),
        compiler_params=_cparams(("parallel", "parallel")),
        name="mod",
    )(c_rows, w_mod, b_mod.reshape(n_layers, 1, n))


def _ffn_kernel(x_ref, mod_ref, wa_ref, wb_ref, wout_ref, o_ref, xn_ref, *, mod_base):
    j = pl.program_id(1)

    @pl.when(j == 0)
    def _():
        xn = _modulated_norm(x_ref[...], mod_ref[mod_base:mod_base + 1, :],
                             mod_ref[mod_base + 1:mod_base + 2, :])
        xn_ref[...] = xn.astype(BF16)
        o_ref[...] = jnp.zeros(o_ref.shape, F32)

    xn = xn_ref[...]
    a = jnp.dot(xn, wa_ref[...], preferred_element_type=F32)
    b = jnp.dot(xn, wb_ref[...], preferred_element_type=F32)
    g = (a * _sigmoid(a) * b).astype(BF16)
    o_ref[...] += jnp.dot(g, wout_ref[...], preferred_element_type=F32)

    @pl.when(j == pl.num_programs(1) - 1)
    def _():
        gate = mod_ref[mod_base + 2:mod_base + 3, :]
        o_ref[...] = x_ref[...] + (0.5 * gate) * o_ref[...]


def _ffn_call(h, mods, w_in, w_out, layer, mod_base, n_rows, tm, group_of):
    d = h.shape[1]
    d_ff = w_out.shape[1]
    tf = _pick_tile(d_ff, TF)
    n_f = d_ff // tf
    return pl.pallas_call(
        functools.partial(_ffn_kernel, mod_base=mod_base),
        grid=(pl.cdiv(n_rows, tm), n_f),
        in_specs=[
            pl.BlockSpec((tm, d), lambda i, j: (i, 0)),
            pl.BlockSpec((None, None, N_MOD, d), lambda i, j: (layer, group_of(i), 0, 0)),
            pl.BlockSpec((None, d, tf), lambda i, j: (layer, 0, j)),
            pl.BlockSpec((None, d, tf), lambda i, j: (layer, 0, n_f + j)),
            pl.BlockSpec((None, tf, d), lambda i, j: (layer, j, 0)),
        ],
        out_specs=pl.BlockSpec((tm, d), lambda i, j: (i, 0)),
        out_shape=jax.ShapeDtypeStruct((n_rows, d), F32),
        scratch_shapes=[pltpu.VMEM((tm, d), BF16)],
        compiler_params=_cparams(("parallel", "arbitrary")),
        name="ffn",
    )(h, mods, w_in, w_in, w_out)


def _rope_b(x, tab_ref):
    return (x * tab_ref[0] + pltpu.roll(x, ROPE_B // 2, 1) * tab_ref[1]
            + pltpu.roll(x, LANES - ROPE_B // 2, 1) * tab_ref[2])


def _rope_c(x, tab_ref):
    return x * tab_ref[3] + pltpu.roll(x, DH_C // 2, 1) * tab_ref[4]


def _prep_kernel(x_ref, mod_ref, wa_ref, wukv_ref, wuq_ref, gv_ref, tab_ref,
                 xn_ref, qb_ref, kb_ref, vb_ref, qc_ref, kc_ref, vc_ref, *, kvl, ql):
    xn = _modulated_norm(x_ref[...], mod_ref[3:4, :], mod_ref[4:5, :]).astype(BF16)
    xn_ref[...] = xn
    p = jnp.dot(xn, wa_ref[...], preferred_element_type=F32)
    o = 0
    ckv = p[:, o:o + kvl]; o += kvl
    kr = p[:, o:o + LANES]; o += LANES
    kc = p[:, o:o + HKV_C * DH_C]; o += HKV_C * DH_C
    vc = p[:, o:o + HKV_C * DH_C]; o += HKV_C * DH_C
    cq = p[:, o:o + ql]; o += ql
    qc = p[:, o:o + HC * DH_C]

    g_kv = gv_ref[0:1, 0:kvl]
    g_ql = gv_ref[1:2, 0:ql]
    gk_nope = gv_ref[2:3, 0:LANES]
    gk_pe = gv_ref[2:3, LANES:2 * LANES]
    gq_nope = gv_ref[2:3, 2 * LANES:3 * LANES]
    gq_pe = gv_ref[2:3, 3 * LANES:4 * LANES]
    g_kc = gv_ref[3:4, 0:LANES]
    g_qc = gv_ref[3:4, LANES:2 * LANES]

    def rms_rows(v):
        return v * lax.rsqrt(jnp.mean(v * v, axis=-1, keepdims=True) + EPS)

    ckv_n = (rms_rows(ckv) * g_kv).astype(BF16)
    kv = jnp.dot(ckv_n, wukv_ref[...], preferred_element_type=F32)
    kr_rot = _rope_b(kr * gk_pe, tab_ref)
    ss_r = jnp.sum(kr * kr, axis=-1, keepdims=True)
    for h in range(HB):
        kn = kv[:, h * NOPE_B:(h + 1) * NOPE_B]
        r = lax.rsqrt((jnp.sum(kn * kn, axis=-1, keepdims=True) + ss_r) * (1.0 / QK_B) + EPS)
        kb_ref[h, 0:NOPE_B, :] = (kn * r * gk_nope).T.astype(BF16)
        kb_ref[h, NOPE_B:NOPE_B + LANES, :] = (kr_rot * r).T.astype(BF16)
        vb_ref[h] = kv[:, HB * NOPE_B + h * V_B:HB * NOPE_B + (h + 1) * V_B].astype(BF16)

    cq_n = (rms_rows(cq) * g_ql).astype(BF16)
    q = jnp.dot(cq_n, wuq_ref[...], preferred_element_type=F32)
    for h in range(HB):
        qn = q[:, h * 2 * LANES:h * 2 * LANES + NOPE_B]
        qp = q[:, h * 2 * LANES + NOPE_B:(h + 1) * 2 * LANES]
        ss = jnp.sum(qn * qn, axis=-1, keepdims=True) + jnp.sum(qp * qp, axis=-1, keepdims=True)
        r = lax.rsqrt(ss * (1.0 / QK_B) + EPS) * (QK_B ** -0.5 * LOG2_E)
        qb_ref[h, :, 0:NOPE_B] = (qn * r * gq_nope).astype(BF16)
        qb_ref[h, :, NOPE_B:NOPE_B + LANES] = (_rope_b(qp * gq_pe, tab_ref) * r).astype(BF16)

    for h in range(HKV_C):
        kh = rms_rows(kc[:, h * DH_C:(h + 1) * DH_C]) * g_kc
        kc_ref[h] = _rope_c(kh, tab_ref).T.astype(BF16)
        vc_ref[h] = vc[:, h * DH_C:(h + 1) * DH_C].astype(BF16)
    for h in range(HC):
        qh = rms_rows(qc[:, h * DH_C:(h + 1) * DH_C]) * g_qc
        qc_ref[h] = (_rope_c(qh, tab_ref) * (DH_C ** -0.5 * LOG2_E)).astype(BF16)


def _prep_call(h, mods, w_all, w_ukv_r, w_uq_r, gvec, tabs, layer, geo, n_a):
    t_rows, d = h.shape
    tm = TM_A
    kvl, ql = w_ukv_r.shape[1], w_uq_r.shape[1]
    bsz, p_len = geo["bsz"], geo["p_len"]
    bp = lambda i: (geo["batch_of"](i), 0, geo["pos_of"](i), 0)
    hd = lambda nh, dd: pl.BlockSpec((None, nh, tm, dd), bp)
    sh = lambda nh, dd: jax.ShapeDtypeStruct((bsz, nh, p_len, dd), BF16)
    bp_t = lambda i: (geo["batch_of"](i), 0, 0, geo["pos_of"](i))
    hd_t = lambda nh, dd: pl.BlockSpec((None, nh, dd, tm), bp_t)
    sh_t = lambda nh, dd: jax.ShapeDtypeStruct((bsz, nh, dd, p_len), BF16)
    return pl.pallas_call(
        functools.partial(_prep_kernel, kvl=kvl, ql=ql),
        grid=(t_rows // tm,),
        in_specs=[
            pl.BlockSpec((tm, d), lambda i: (i, 0)),
            pl.BlockSpec((None, None, N_MOD, d), lambda i: (layer, geo["group_a"](i), 0, 0)),
            pl.BlockSpec((None, d, n_a), lambda i: (layer, 0, 0)),
            pl.BlockSpec((None, kvl, w_ukv_r.shape[2]), lambda i: (layer, 0, 0)),
            pl.BlockSpec((None, ql, w_uq_r.shape[2]), lambda i: (layer, 0, 0)),
            pl.BlockSpec((None, SUBLANES, gvec.shape[2]), lambda i: (layer, 0, 0)),
            pl.BlockSpec((5, tm, LANES), lambda i: (0, geo["pos_of"](i), 0)),
        ],
        out_specs=[
            pl.BlockSpec((tm, d), lambda i: (i, 0)),
            hd(HB, 2 * LANES), hd_t(HB, 2 * LANES), hd(HB, V_B),
            hd(HC, DH_C), hd_t(HKV_C, DH_C), hd(HKV_C, DH_C),
        ],
        out_shape=[
            jax.ShapeDtypeStruct((t_rows, d), BF16),
            sh(HB, 2 * LANES), sh_t(HB, 2 * LANES), sh(HB, V_B),
            sh(HC, DH_C), sh_t(HKV_C, DH_C), sh(HKV_C, DH_C),
        ],
        compiler_params=_cparams(("parallel",)),
        name="attn_prep",
    )(h, mods, w_all, w_ukv_r, w_uq_r, gvec, tabs)


def _mm_kernel(a_ref, w_ref, o_ref):
    o_ref[...] = jnp.dot(a_ref[...], w_ref[...], preferred_element_type=F32)


def _mix_proj_call(xn, w_all, layer, n_rows, col0, n):
    d = xn.shape[1]
    tm, tn = min(TM_MIX, n_rows), _pick_tile(n, TN_MIX)
    assert col0 % tn == 0
    blk0 = col0 // tn
    return pl.pallas_call(
        _mm_kernel,
        grid=(pl.cdiv(n_rows, tm), n // tn),
        in_specs=[
            pl.BlockSpec((tm, d), lambda i, j: (i, 0)),
            pl.BlockSpec((None, d, tn), lambda i, j: (layer, 0, blk0 + j)),
        ],
        out_specs=pl.BlockSpec((tm, tn), lambda i, j: (i, j)),
        out_shape=jax.ShapeDtypeStruct((n_rows, n), F32),
        compiler_params=_cparams(("parallel", "parallel")),
        name="mix_proj",
    )(xn, w_all)


def _attn_kernel(q_ref, k_ref, v_ref, o_ref, s0_ref, s1_ref, p_ref, alpha_ref, m_ref, l_ref,
                 acc_ref, *,
                 group, tq, dv, key_lo, key_hi):
    rows = group * tq
    q = q_ref[...].reshape(rows, q_ref.shape[-1])
    m_ref[...] = jnp.full(m_ref.shape, -jnp.inf, F32)
    l_ref[...] = jnp.zeros(l_ref.shape, F32)
    acc_ref[...] = jnp.zeros(acc_ref.shape, F32)

    def scores(off, width, s_ref):
        s_ref[:, 0:width] = jnp.dot(q, k_ref[:, pl.ds(off, width)], preferred_element_type=F32)

    def update(off, width, s_ref):
        for rb in range(rows // ATTN_ROW_BLOCK):
            sl = slice(rb * ATTN_ROW_BLOCK, (rb + 1) * ATTN_ROW_BLOCK)
            parts = [s_ref[sl, i * LANES:(i + 1) * LANES] for i in range(width // LANES)]
            m_prev = m_ref[sl, :]
            m_cur = functools.reduce(jnp.maximum, parts)
            m_new = jnp.maximum(m_prev, jnp.max(m_cur, axis=-1, keepdims=True))
            alpha = jnp.exp2(m_prev - m_new)
            p_parts = [jnp.exp2(part - m_new) for part in parts]
            l_ref[sl, :] = alpha * l_ref[sl, :] + functools.reduce(jnp.add, p_parts)
            for i, pp in enumerate(p_parts):
                p_ref[sl, i * LANES:(i + 1) * LANES] = pp.astype(BF16)
            alpha_ref[sl, :] = alpha
            m_ref[sl, :] = m_new
        pv = jnp.dot(p_ref[:, 0:width], v_ref[pl.ds(off, width), :],
                     preferred_element_type=F32)
        acc_ref[...] = alpha_ref[...] * acc_ref[...] + pv

    n_full, tail = divmod(key_hi - key_lo, TK)
    n_chunks = n_full + (1 if tail else 0)
    bufs = (s0_ref, s1_ref)

    def chunk(idx):
        return key_lo + idx * TK, (TK if idx < n_full else tail)

    per_trip = 2 * ATTN_PAIRS_PER_TRIP
    n_trips = max(0, (n_full - 1) // per_trip)
    scores(*chunk(0), s0_ref)

    def trip(i, carry):
        off = pl.multiple_of(key_lo + per_trip * i * TK, TK)
        for k in range(per_trip):
            scores(off + (k + 1) * TK, TK, bufs[(k + 1) % 2])
            update(off + k * TK, TK, bufs[k % 2])
        return carry

    lax.fori_loop(0, n_trips, trip, 0)
    for idx in range(per_trip * n_trips, n_chunks):
        if idx + 1 < n_chunks:
            scores(*chunk(idx + 1), bufs[(idx + 1) % 2])
        update(*chunk(idx), bufs[idx % 2])
    out = acc_ref[...] / jnp.sum(l_ref[...], axis=-1, keepdims=True)
    for g in range(group):
        o_ref[:, g * dv:(g + 1) * dv] = out[g * tq:(g + 1) * tq].astype(o_ref.dtype)


def _attn_call(q, k, v, group, tq, q_pos0, q_len, key_lo, key_hi):
    bsz, n_heads, p_len, dk = q.shape
    n_kv = k.shape[1]
    dv = v.shape[-1]
    assert dv == LANES and n_heads == n_kv * group
    assert q_pos0 % tq == 0 and q_len % tq == 0
    assert key_lo % TK == 0 and (key_hi - key_lo) % LANES == 0
    q_tiles = q_len // tq
    q_blk0 = q_pos0 // tq
    return pl.pallas_call(
        functools.partial(_attn_kernel, group=group, tq=tq, dv=dv, key_lo=key_lo, key_hi=key_hi),
        grid=(bsz, n_kv, q_tiles),
        in_specs=[
            pl.BlockSpec((None, group, tq, dk), lambda b, h, qi: (b, h, q_blk0 + qi, 0)),
            pl.BlockSpec((None, None, dk, p_len), lambda b, h, qi: (b, h, 0, 0)),
            pl.BlockSpec((None, None, p_len, dv), lambda b, h, qi: (b, h, 0, 0)),
        ],
        out_specs=pl.BlockSpec((tq, group * dv), lambda b, h, qi: (b * q_tiles + qi, h)),
        out_shape=jax.ShapeDtypeStruct((bsz * q_len, n_heads * dv), BF16),
        scratch_shapes=[
            pltpu.VMEM((group * tq, TK), F32),
            pltpu.VMEM((group * tq, TK), F32),
            pltpu.VMEM((group * tq, TK), BF16),
            pltpu.VMEM((group * tq, LANES), F32),
            pltpu.VMEM((group * tq, LANES), F32),
            pltpu.VMEM((group * tq, LANES), F32),
            pltpu.VMEM((group * tq, dv), F32),
        ],
        compiler_params=_cparams(("parallel", "parallel", "arbitrary")),
        name="attn_g%d_q%d" % (group, tq),
    )(q, k, v)


def _attention(q, k, v, group, tq_lat, geo, with_ctx_queries):
    seq, ctx_len, p_len = geo["seq"], geo["ctx"], geo["p_len"]
    out = _attn_call(q, k, v, group, tq_lat, 0, seq, 0, p_len)
    if with_ctx_queries:
        out_ctx = _attn_call(q, k, v, group, TM_A, seq, ctx_len, seq, p_len)
        out = jnp.concatenate([out, out_ctx], axis=0)
    return out


def _conv_a_kernel(cur_ref, prev_ref, next_ref, w_ref, vec_ref, o_ref, ext_ref, y_ref, *,
                   d_a, tm, first_of, last_of):
    i = pl.program_id(0)

    def glu(u):
        return u[:, :d_a] * _sigmoid(u[:, d_a:])

    keep_prev = jnp.where(first_of(i), 0.0, 1.0)
    keep_next = jnp.where(last_of(i), 0.0, 1.0)
    ext_ref[0, 0:HALO_A, :] = glu(prev_ref[...]) * keep_prev
    ext_ref[0, HALO_A:HALO_A + tm, :] = glu(cur_ref[...])
    ext_ref[0, HALO_A + tm:2 * HALO_A + tm, :] = glu(next_ref[...]) * keep_next
    span = tm + 2 * HALO_A - SUBLANES
    for r in range(1, SUBLANES):
        ext_ref[r, 0:span, :] = ext_ref[0, r:r + span, :]

    def lane_chunk(c, carry):
        cs = pl.multiple_of(c * CONV_LANES, CONV_LANES)
        for rc in range(tm // CONV_ROWS):
            acc = jnp.zeros((CONV_ROWS, CONV_LANES), F32)
            for tap in range(CONV_A):
                s = tap + HALO_A - CONV_A // 2
                q8, r = divmod(s, SUBLANES)
                row0 = q8 * SUBLANES + rc * CONV_ROWS
                acc = acc + (w_ref[tap:tap + 1, pl.ds(cs, CONV_LANES)]
                             * ext_ref[r, row0:row0 + CONV_ROWS, pl.ds(cs, CONV_LANES)])
            y_ref[rc * CONV_ROWS:(rc + 1) * CONV_ROWS, pl.ds(cs, CONV_LANES)] = acc
        return carry

    lax.fori_loop(0, d_a // CONV_LANES, lane_chunk, 0)
    y = y_ref[...] + vec_ref[0:1, :]
    mu = jnp.mean(y, axis=-1, keepdims=True)
    yc = y - mu
    var = jnp.mean(yc * yc, axis=-1, keepdims=True)
    z = yc * lax.rsqrt(var + EPS) * vec_ref[1:2, :] + vec_ref[2:3, :]
    o_ref[...] = (z * _sigmoid(z)).astype(o_ref.dtype)


def _conv_a_call(p2, w_dw, vecs, layer, n_rows, geo):
    d_a = w_dw.shape[2]
    tm = TM_A
    hb = tm // HALO_A
    n_halo_blocks = p2.shape[0] // HALO_A
    return pl.pallas_call(
        functools.partial(_conv_a_kernel, d_a=d_a, tm=tm,
                          first_of=geo["first_a"], last_of=geo["last_a"]),
        grid=(n_rows // tm,),
        in_specs=[
            pl.BlockSpec((tm, 2 * d_a), lambda i: (i, 0)),
            pl.BlockSpec((HALO_A, 2 * d_a), lambda i: (jnp.maximum(i * hb - 1, 0), 0)),
            pl.BlockSpec((HALO_A, 2 * d_a),
                         lambda i: (jnp.minimum((i + 1) * hb, n_halo_blocks - 1), 0)),
            pl.BlockSpec((None, CONV_A, d_a), lambda i: (layer, 0, 0)),
            pl.BlockSpec((None, SUBLANES, d_a), lambda i: (layer, 0, 0)),
        ],
        out_specs=pl.BlockSpec((tm, d_a), lambda i: (i, 0)),
        out_shape=jax.ShapeDtypeStruct((n_rows, d_a), BF16),
        scratch_shapes=[
            pltpu.VMEM((SUBLANES, tm + 2 * HALO_A, d_a), F32),
            pltpu.VMEM((tm, d_a), F32),
        ],
        compiler_params=_cparams(("parallel",)),
        name="conv_a",
    )(p2, p2, p2, w_dw, vecs)


def _conv_d_kernel(bg_ref, cg_ref, hh_ref, cg_prev_ref, hh_prev_ref, cg_next_ref, hh_next_ref,
                   w_ref, o_ref, ext_ref, *, tm, first_of, last_of):
    i = pl.program_id(0)
    keep_prev = jnp.where(first_of(i), 0.0, 1.0)
    keep_next = jnp.where(last_of(i), 0.0, 1.0)
    ext_ref[0:HALO_D, :] = cg_prev_ref[...] * hh_prev_ref[...] * keep_prev
    ext_ref[HALO_D:HALO_D + tm, :] = cg_ref[...] * hh_ref[...]
    ext_ref[HALO_D + tm:2 * HALO_D + tm, :] = cg_next_ref[...] * hh_next_ref[...] * keep_next
    acc = None
    for tap in range(CONV_D):
        s = tap + HALO_D - CONV_D // 2
        term = w_ref[tap:tap + 1, :] * ext_ref[s:s + tm, :]
        acc = term if acc is None else acc + term
    o_ref[...] = (bg_ref[...] * acc).astype(o_ref.dtype)


def _conv_d_call(p2, w_dw, layer, n_rows, col0, geo):
    d_d = w_dw.shape[2]
    tm = TM_A
    hb = tm // HALO_D
    n_halo_blocks = p2.shape[0] // HALO_D
    cb = col0 // d_d
    cur = lambda k: pl.BlockSpec((tm, d_d), lambda i: (i, cb + k))
    prev = lambda k: pl.BlockSpec((HALO_D, d_d), lambda i: (jnp.maximum(i * hb - 1, 0), cb + k))
    nxt = lambda k: pl.BlockSpec(
        (HALO_D, d_d), lambda i: (jnp.minimum((i + 1) * hb, n_halo_blocks - 1), cb + k))
    return pl.pallas_call(
        functools.partial(_conv_d_kernel, tm=tm, first_of=geo["first_a"], last_of=geo["last_a"]),
        grid=(n_rows // tm,),
        in_specs=[cur(0), cur(1), cur(2), prev(1), prev(2), nxt(1), nxt(2),
                  pl.BlockSpec((None, SUBLANES, d_d), lambda i: (layer, 0, 0))],
        out_specs=pl.BlockSpec((tm, d_d), lambda i: (i, 0)),
        out_shape=jax.ShapeDtypeStruct((n_rows, d_d), BF16),
        scratch_shapes=[pltpu.VMEM((tm + 2 * HALO_D, d_d), F32)],
        compiler_params=_cparams(("parallel",)),
        name="conv_d",
    )(p2, p2, p2, p2, p2, p2, p2, w_dw)


def _merge_kernel(h_ref, mod_ref, xn_ref, aa_ref, ab_ref, ac_ref, ad_ref,
                  wg0_ref, wg1_ref, wg2_ref, wg3_ref, wa_ref, wb_ref, wc_ref, wd_ref,
                  wo_ref, o_ref):
    j = pl.program_id(1)

    @pl.when(j == 0)
    def _():
        o_ref[...] = jnp.zeros(o_ref.shape, F32)

    xn = xn_ref[...]
    y = None
    for a_ref, wg_ref, wbr_ref in ((aa_ref, wg0_ref, wa_ref), (ab_ref, wg1_ref, wb_ref),
                                   (ac_ref, wg2_ref, wc_ref), (ad_ref, wg3_ref, wd_ref)):
        gl = jnp.dot(xn, wg_ref[...], preferred_element_type=F32)
        yb = jnp.dot(a_ref[...], wbr_ref[...], preferred_element_type=F32)
        term = _sigmoid(gl) * yb
        y = term if y is None else y + term
    o_ref[...] += jnp.dot(y.astype(BF16), wo_ref[...], preferred_element_type=F32)

    @pl.when(j == pl.num_programs(1) - 1)
    def _():
        o_ref[...] = h_ref[...] + mod_ref[5:6, :] * o_ref[...]


def _merge_call(h, mods, xn, acts, w_gates, gate_col0, w_branches, w_o, layer, n_rows, group_of):
    d = h.shape[1]
    tn = _pick_tile(d, TN_MERGE)
    n_j = d // tn
    assert gate_col0 % tn == 0
    g0 = gate_col0 // tn
    d_br = w_branches[0].shape[1]
    tm = TM_F
    act = pl.BlockSpec((tm, d_br), lambda i, j: (i, 0))
    gate = lambda b: pl.BlockSpec((None, d, tn), lambda i, j: (layer, 0, g0 + b * n_j + j))
    branch = pl.BlockSpec((None, d_br, tn), lambda i, j: (layer, 0, j))
    return pl.pallas_call(
        _merge_kernel,
        grid=(n_rows // tm, n_j),
        in_specs=[
            pl.BlockSpec((tm, d), lambda i, j: (i, 0)),
            pl.BlockSpec((None, None, N_MOD, d), lambda i, j: (layer, group_of(i), 0, 0)),
            pl.BlockSpec((tm, d), lambda i, j: (i, 0)),
            act, act, act, act,
            gate(0), gate(1), gate(2), gate(3),
            branch, branch, branch, branch,
            pl.BlockSpec((None, tn, d), lambda i, j: (layer, j, 0)),
        ],
        out_specs=pl.BlockSpec((tm, d), lambda i, j: (i, 0)),
        out_shape=jax.ShapeDtypeStruct((n_rows, d), F32),
        compiler_params=_cparams(("parallel", "arbitrary")),
        name="merge",
    )(h, mods, xn, *acts, w_gates, w_gates, w_gates, w_gates, *w_branches, w_o)


def _rope_tables(seq, ctx):
    t = jnp.arange(seq)
    row = (t // GRID_W).astype(F32)
    col = (t % GRID_W).astype(F32)

    def cos_sin(d_rot):
        n_freq = d_rot // 4
        inv = ROPE_BASE ** (-jnp.arange(n_freq, dtype=F32) / n_freq)
        ang = jnp.concatenate([row[:, None] * inv, col[:, None] * inv], axis=-1)
        pad = ((0, ctx), (0, 0))
        return (jnp.pad(jnp.cos(ang), pad, constant_values=1.0), jnp.pad(jnp.sin(ang), pad))

    cb, sb = cos_sin(ROPE_B)
    cc, sc = cos_sin(DH_C)
    zb = jnp.zeros_like(sb)
    fill = jnp.zeros((seq + ctx, LANES - ROPE_B), F32)
    return jnp.stack([
        jnp.concatenate([cb, cb, fill], axis=-1),
        jnp.concatenate([zb, sb, fill], axis=-1),
        jnp.concatenate([-sb, zb, fill], axis=-1),
        jnp.concatenate([cc, cc], axis=-1),
        jnp.concatenate([-sc, sc], axis=-1),
    ])


def kernel(x, c, ctx, c_ctx, w_mod, b_mod, w_ffn1_in, w_ffn1_out, w_ffn2_in, w_ffn2_out,
           w_in, g_q_lora, w_uq, g_kv_lora, w_ukv, g_q_b, g_k_b, w_o_b, g_q_c, g_k_c, w_o_c,
           w_dw_a, b_dw_a, g_ln_a, b_ln_a, w_out_a, w_dw_d, w_out_d, w_o):
    bsz, seq, d = x.shape
    ctx_len = ctx.shape[1]
    depth = w_mod.shape[0]
    d_ff = w_ffn1_out.shape[1]
    ql, kvl = g_q_lora.shape[1], g_kv_lora.shape[1]
    d_a, d_d = w_dw_a.shape[2], w_dw_d.shape[2]
    n_lat = bsz * seq
    t_rows = n_lat + bsz * ctx_len
    p_len = seq + ctx_len
    assert seq % TM_F == 0 and (bsz * ctx_len) % TM_F == 0 and ctx_len % TM_A == 0
    assert seq % TK == 0 and ctx_len % LANES == 0 and bsz + 1 <= SUBLANES
    assert d_a == d_d == HB * V_B == HC * DH_C

    tps_a, tps_f = seq // TM_A, seq // TM_F
    nlt_a = bsz * tps_a
    ncb_a = ctx_len // TM_A
    geo = dict(
        bsz=bsz, seq=seq, ctx=ctx_len, p_len=p_len, n_lat=n_lat, t_rows=t_rows,
        group_a=lambda i: jnp.minimum(i // tps_a, bsz),
        batch_of=lambda i: jnp.where(i < nlt_a, i // tps_a, (i - nlt_a) // ncb_a),
        pos_of=lambda i: jnp.where(i < nlt_a, i % tps_a, tps_a + (i - nlt_a) % ncb_a),
        first_a=lambda i: jnp.where(i < nlt_a, i % tps_a == 0, (i - nlt_a) % ncb_a == 0),
        last_a=lambda i: jnp.where(i < nlt_a, i % tps_a == tps_a - 1,
                                   (i - nlt_a) % ncb_a == ncb_a - 1),
    )
    group_f = lambda i: jnp.minimum(i // tps_f, bsz)
    tm_ffn = min(TM_FFN, seq)
    assert seq % tm_ffn == 0 and bsz * ctx_len <= tm_ffn
    group_ffn = lambda i: jnp.minimum(i // (seq // tm_ffn), bsz)

    c_rows = jnp.concatenate(
        [c, c_ctx[None, :], jnp.zeros((SUBLANES - bsz - 1, d), F32)], axis=0)
    mods = _mod_call(c_rows, w_mod, b_mod).reshape(depth, SUBLANES, N_MOD, d)

    w1_in, w2_in = w_ffn1_in.astype(BF16), w_ffn2_in.astype(BF16)
    w1_out, w2_out = w_ffn1_out.astype(BF16), w_ffn2_out.astype(BF16)

    kv_cols = kvl + ROPE_B + 2 * HKV_C * DH_C
    attn_cols = kv_cols + ql + HC * DH_C
    mix_cols = 2 * d_a + 3 * d_d
    n_a = attn_cols + LANES - ROPE_B
    mix0 = _round_up(n_a, _pick_tile(mix_cols, TN_MIX))
    gate0 = _round_up(mix0 + mix_cols, _pick_tile(d, TN_MERGE))
    zcols = lambda n: jnp.zeros((depth, d, n), BF16)
    w_all = jnp.concatenate(
        [w_in[:, :, :kvl + ROPE_B].astype(BF16), zcols(LANES - ROPE_B),
         w_in[:, :, kvl + ROPE_B:attn_cols].astype(BF16), zcols(mix0 - n_a),
         w_in[:, :, attn_cols:attn_cols + mix_cols].astype(BF16),
         zcols(gate0 - mix0 - mix_cols),
         w_in[:, :, attn_cols + mix_cols:].astype(BF16)], axis=-1)
    w_branches = [w.astype(BF16) for w in (w_out_a, w_o_b, w_o_c, w_out_d)]
    w_o_bf = w_o.astype(BF16)

    w_ukv_r = w_ukv.astype(BF16).reshape(depth, kvl, HB, 2, NOPE_B)
    w_ukv_r = jnp.transpose(w_ukv_r, (0, 1, 3, 2, 4)).reshape(depth, kvl, 2 * HB * NOPE_B)
    w_uq_r = jnp.pad(w_uq.astype(BF16).reshape(depth, ql, HB, QK_B),
                     ((0, 0), (0, 0), (0, 0), (0, 2 * LANES - QK_B)))
    w_uq_r = w_uq_r.reshape(depth, ql, HB * 2 * LANES)

    gw = max(kvl, ql, 4 * LANES)
    padw = lambda v: jnp.pad(v, ((0, 0), (0, gw - v.shape[1])))
    pad_pe = lambda g: jnp.pad(g[:, NOPE_B:], ((0, 0), (0, LANES - ROPE_B)))
    gvec = jnp.stack([
        padw(g_kv_lora), padw(g_q_lora),
        padw(jnp.concatenate([g_k_b[:, :NOPE_B], pad_pe(g_k_b),
                              g_q_b[:, :NOPE_B], pad_pe(g_q_b)], axis=-1)),
        padw(jnp.concatenate([g_k_c, g_q_c], axis=-1)),
    ] + [jnp.zeros((depth, gw), F32)] * (SUBLANES - 4), axis=1)
    vecs_a = jnp.stack([b_dw_a, g_ln_a, b_ln_a]
                       + [jnp.zeros_like(b_dw_a)] * (SUBLANES - 3), axis=1)
    w_dw_d_p = jnp.pad(w_dw_d, ((0, 0), (0, SUBLANES - CONV_D), (0, 0)))
    tabs = _rope_tables(seq, ctx_len)

    h = jnp.concatenate([x.reshape(n_lat, d), ctx.reshape(bsz * ctx_len, d)], axis=0)
    for layer in range(depth):
        last = layer == depth - 1
        rows_out = n_lat if last else t_rows
        h = _ffn_call(h, mods, w1_in, w1_out, layer, 0, t_rows, tm_ffn, group_ffn)
        xn, qb, kb, vb, qc, kc, vc = _prep_call(
            h, mods, w_all, w_ukv_r, w_uq_r, gvec, tabs, layer, geo, n_a)
        p2 = _mix_proj_call(xn, w_all, layer, t_rows, mix0, mix_cols)
        ab = _attention(qb, kb, vb, 1, min(TQ_ROWS, seq), geo, not last)
        ac = _attention(qc, kc, vc, GROUP_C, min(TQ_ROWS // GROUP_C, seq), geo, not last)
        aa = _conv_a_call(p2, w_dw_a, vecs_a, layer, rows_out, geo)
        ad = _conv_d_call(p2, w_dw_d_p, layer, rows_out, 2 * d_a, geo)
        h = _merge_call(h, mods, xn, (aa, ab, ac, ad), w_all, gate0, w_branches, w_o_bf,
                        layer, rows_out, group_f)
        h = _ffn_call(h, mods, w2_in, w2_out, layer, 6, rows_out, tm_ffn, group_ffn)
    return h.reshape(bsz, seq, d)
```

```python
import functools

import jax
import jax.numpy as jnp
from jax import lax
from jax.experimental import pallas as pl
from jax.experimental.pallas import tpu as pltpu

F32 = jnp.float32
BF16 = jnp.bfloat16

GRID_W = 64
N_MOD = 9
HB = 8
NOPE_B = 128
ROPE_B = 64
V_B = 128
QK_B = NOPE_B + ROPE_B
HC = 8
HKV_C = 2
DH_C = 128
GROUP_C = HC // HKV_C
CONV_A = 31
CONV_D = 3
N_BRANCH = 4
ROPE_BASE = 10000.0
EPS = 1e-6

LANES = 128
SUBLANES = 8
VMEM_LIMIT = 56 * 1024 * 1024

TM_A = 256
TM_F = 512
TM_FFN = 512
TF = 512
TN_MERGE = 256
TN_MOD = 1024
TN_MIX = 512
TM_MIX = 1024
TK = 256
TQ_ROWS = 1024
ATTN_ROW_BLOCK = 128
ATTN_PAIRS_PER_TRIP = 8
LOG2_E = 1.4426950408889634
HALO_A = 16
HALO_D = 8
CONV_ROWS = 64
CONV_LANES = 128


def _cparams(sem):
    return pltpu.CompilerParams(dimension_semantics=sem, vmem_limit_bytes=VMEM_LIMIT)


def _pick_tile(n, cap):
    t = min(cap, n) // LANES * LANES
    while n % t:
        t -= LANES
    return t


def _round_up(n, m):
    return -(-n // m) * m


def _sigmoid(x):
    return 1.0 / (1.0 + jnp.exp(-x))


def _modulated_norm(x, shift, scale):
    y = x * lax.rsqrt(jnp.mean(x * x, axis=-1, keepdims=True) + EPS)
    return y * (1.0 + scale) + shift


def _mod_kernel(c_ref, w_ref, b_ref, o_ref):
    c = c_ref[...]
    s = c * _sigmoid(c)
    o_ref[...] = jnp.dot(s, w_ref[...], preferred_element_type=F32) + b_ref[...]


def _mod_call(c_rows, w_mod, b_mod):
    n_layers, d, n = w_mod.shape
    tn = _pick_tile(n, TN_MOD)
    return pl.pallas_call(
        _mod_kernel,
        grid=(n_layers, n // tn),
        in_specs=[
            pl.BlockSpec((SUBLANES, d), lambda l, j: (0, 0)),
            pl.BlockSpec((None, d, tn), lambda l, j: (l, 0, j)),
            pl.BlockSpec((None, 1, tn), lambda l, j: (l, 0, j)),
        ],
        out_specs=pl.BlockSpec((None, SUBLANES, tn), lambda l, j: (l, 0, j)),
        out_shape=jax.ShapeDtypeStruct((n_layers, SUBLANES, n), F32),
        compiler_params=_cparams(("parallel", "parallel")),
        name="mod",
    )(c_rows, w_mod, b_mod.reshape(n_layers, 1, n))


def _ffn_kernel(x_ref, mod_ref, wa_ref, wb_ref, wout_ref, o_ref, xn_ref, *, mod_base):
    j = pl.program_id(1)

    @pl.when(j == 0)
    def _():
        xn = _modulated_norm(x_ref[...], mod_ref[mod_base:mod_base + 1, :],
                             mod_ref[mod_base + 1:mod_base + 2, :])
        xn_ref[...] = xn.astype(BF16)
        o_ref[...] = jnp.zeros(o_ref.shape, F32)

    xn = xn_ref[...]
    a = jnp.dot(xn, wa_ref[...], preferred_element_type=F32)
    b = jnp.dot(xn, wb_ref[...], preferred_element_type=F32)
    g = (a * _sigmoid(a) * b).astype(BF16)
    o_ref[...] += jnp.dot(g, wout_ref[...], preferred_element_type=F32)

    @pl.when(j == pl.num_programs(1) - 1)
    def _():
        gate = mod_ref[mod_base + 2:mod_base + 3, :]
        o_ref[...] = x_ref[...] + (0.5 * gate) * o_ref[...]


def _ffn_call(h, mods, w_in, w_out, layer, mod_base, n_rows, tm, group_of):
    d = h.shape[1]
    d_ff = w_out.shape[1]
    tf = _pick_tile(d_ff, TF)
    n_f = d_ff // tf
    return pl.pallas_call(
        functools.partial(_ffn_kernel, mod_base=mod_base),
        grid=(pl.cdiv(n_rows, tm), n_f),
        in_specs=[
            pl.BlockSpec((tm, d), lambda i, j: (i, 0)),
            pl.BlockSpec((None, None, N_MOD, d), lambda i, j: (layer, group_of(i), 0, 0)),
            pl.BlockSpec((None, d, tf), lambda i, j: (layer, 0, j)),
            pl.BlockSpec((None, d, tf), lambda i, j: (layer, 0, n_f + j)),
            pl.BlockSpec((None, tf, d), lambda i, j: (layer, j, 0)),
        ],
        out_specs=pl.BlockSpec((tm, d), lambda i, j: (i, 0)),
        out_shape=jax.ShapeDtypeStruct((n_rows, d), F32),
        scratch_shapes=[pltpu.VMEM((tm, d), BF16)],
        compiler_params=_cparams(("parallel", "arbitrary")),
        name="ffn",
    )(h, mods, w_in, w_in, w_out)


def _rope_b(x, tab_ref):
    return (x * tab_ref[0] + pltpu.roll(x, ROPE_B // 2, 1) * tab_ref[1]
            + pltpu.roll(x, LANES - ROPE_B // 2, 1) * tab_ref[2])


def _rope_c(x, tab_ref):
    return x * tab_ref[3] + pltpu.roll(x, DH_C // 2, 1) * tab_ref[4]


def _prep_kernel(x_ref, mod_ref, wh_ref, wa_ref, wukv_ref, wuq_ref, gv_ref, tab_ref,
                 xn_ref, qb_ref, kb_ref, vb_ref, qc_ref, kc_ref, vc_ref, *, kvl, ql):
    xn = _modulated_norm(x_ref[...], mod_ref[3:4, :], mod_ref[4:5, :]).astype(BF16)
    xn_ref[...] = xn
    ph = jnp.dot(xn, wh_ref[...], preferred_element_type=F32)
    p = jnp.dot(xn, wa_ref[...], preferred_element_type=F32)
    ckv = ph[:, 0:kvl]
    kr = ph[:, kvl:kvl + LANES]
    o = 0
    kc = p[:, o:o + HKV_C * DH_C]; o += HKV_C * DH_C
    vc = p[:, o:o + HKV_C * DH_C]; o += HKV_C * DH_C
    cq = p[:, o:o + ql]; o += ql
    qc = p[:, o:o + HC * DH_C]

    g_kv = gv_ref[0:1, 0:kvl]
    g_ql = gv_ref[1:2, 0:ql]
    gk_nope = gv_ref[2:3, 0:LANES]
    gk_pe = gv_ref[2:3, LANES:2 * LANES]
    gq_nope = gv_ref[2:3, 2 * LANES:3 * LANES]
    gq_pe = gv_ref[2:3, 3 * LANES:4 * LANES]
    g_kc = gv_ref[3:4, 0:LANES]
    g_qc = gv_ref[3:4, LANES:2 * LANES]

    def rms_rows(v):
        return v * lax.rsqrt(jnp.mean(v * v, axis=-1, keepdims=True) + EPS)

    ckv_n = (rms_rows(ckv) * g_kv).astype(BF16)
    kv = jnp.dot(ckv_n, wukv_ref[...], preferred_element_type=F32)
    kr_rot = _rope_b(kr * gk_pe, tab_ref)
    ss_r = jnp.sum(kr * kr, axis=-1, keepdims=True)
    for h in range(HB):
        kn = kv[:, h * NOPE_B:(h + 1) * NOPE_B]
        r = lax.rsqrt((jnp.sum(kn * kn, axis=-1, keepdims=True) + ss_r) * (1.0 / QK_B) + EPS)
        kb_ref[h, 0:NOPE_B, :] = (kn * r * gk_nope).T.astype(BF16)
        kb_ref[h, NOPE_B:NOPE_B + LANES, :] = (kr_rot * r).T.astype(BF16)
        vb_ref[h] = kv[:, HB * NOPE_B + h * V_B:HB * NOPE_B + (h + 1) * V_B].astype(BF16)

    cq_n = (rms_rows(cq) * g_ql).astype(BF16)
    q = jnp.dot(cq_n, wuq_ref[...], preferred_element_type=F32)
    for h in range(HB):
        qn = q[:, h * 2 * LANES:h * 2 * LANES + NOPE_B]
        qp = q[:, h * 2 * LANES + NOPE_B:(h + 1) * 2 * LANES]
        ss = jnp.sum(qn * qn, axis=-1, keepdims=True) + jnp.sum(qp * qp, axis=-1, keepdims=True)
        r = lax.rsqrt(ss * (1.0 / QK_B) + EPS) * (QK_B ** -0.5 * LOG2_E)
        qb_ref[h, :, 0:NOPE_B] = (qn * r * gq_nope).astype(BF16)
        qb_ref[h, :, NOPE_B:NOPE_B + LANES] = (_rope_b(qp * gq_pe, tab_ref) * r).astype(BF16)

    for h in range(HKV_C):
        kh = rms_rows(kc[:, h * DH_C:(h + 1) * DH_C]) * g_kc
        kc_ref[h] = _rope_c(kh, tab_ref).T.astype(BF16)
        vc_ref[h] = vc[:, h * DH_C:(h + 1) * DH_C].astype(BF16)
    for h in range(HC):
        qh = rms_rows(qc[:, h * DH_C:(h + 1) * DH_C]) * g_qc
        qc_ref[h] = (_rope_c(qh, tab_ref) * (DH_C ** -0.5 * LOG2_E)).astype(BF16)


def _prep_call(h, mods, w_head, w_all, w_ukv_r, w_uq_r, gvec, tabs, layer, geo, n_a):
    t_rows, d = h.shape
    tm = TM_A
    kvl, ql = w_ukv_r.shape[1], w_uq_r.shape[1]
    bsz, p_len = geo["bsz"], geo["p_len"]
    bp = lambda i: (geo["batch_of"](i), 0, geo["pos_of"](i), 0)
    hd = lambda nh, dd: pl.BlockSpec((None, nh, tm, dd), bp)
    sh = lambda nh, dd: jax.ShapeDtypeStruct((bsz, nh, p_len, dd), BF16)
    bp_t = lambda i: (geo["batch_of"](i), 0, 0, geo["pos_of"](i))
    hd_t = lambda nh, dd: pl.BlockSpec((None, nh, dd, tm), bp_t)
    sh_t = lambda nh, dd: jax.ShapeDtypeStruct((bsz, nh, dd, p_len), BF16)
    return pl.pallas_call(
        functools.partial(_prep_kernel, kvl=kvl, ql=ql),
        grid=(t_rows // tm,),
        in_specs=[
            pl.BlockSpec((tm, d), lambda i: (i, 0)),
            pl.BlockSpec((None, None, N_MOD, d), lambda i: (layer, geo["group_a"](i), 0, 0)),
            pl.BlockSpec((None, d, w_head.shape[2]), lambda i: (layer, 0, 0)),
            pl.BlockSpec((None, d, n_a), lambda i: (layer, 0, 0)),
            pl.BlockSpec((None, kvl, w_ukv_r.shape[2]), lambda i: (layer, 0, 0)),
            pl.BlockSpec((None, ql, w_uq_r.shape[2]), lambda i: (layer, 0, 0)),
            pl.BlockSpec((None, SUBLANES, gvec.shape[2]), lambda i: (layer, 0, 0)),
            pl.BlockSpec((5, tm, LANES), lambda i: (0, geo["pos_of"](i), 0)),
        ],
        out_specs=[
            pl.BlockSpec((tm, d), lambda i: (i, 0)),
            hd(HB, 2 * LANES), hd_t(HB, 2 * LANES), hd(HB, V_B),
            hd(HC, DH_C), hd_t(HKV_C, DH_C), hd(HKV_C, DH_C),
        ],
        out_shape=[
            jax.ShapeDtypeStruct((t_rows, d), BF16),
            sh(HB, 2 * LANES), sh_t(HB, 2 * LANES), sh(HB, V_B),
            sh(HC, DH_C), sh_t(HKV_C, DH_C), sh(HKV_C, DH_C),
        ],
        compiler_params=_cparams(("parallel",)),
        name="attn_prep",
    )(h, mods, w_head, w_all, w_ukv_r, w_uq_r, gvec, tabs)


def _mm_kernel(a_ref, w_ref, o_ref):
    o_ref[...] = jnp.dot(a_ref[...], w_ref[...], preferred_element_type=F32)


def _mix_proj_call(xn, w_all, layer, n_rows, col0, n):
    d = xn.shape[1]
    tm, tn = min(TM_MIX, n_rows), _pick_tile(n, TN_MIX)
    assert col0 % tn == 0
    blk0 = col0 // tn
    return pl.pallas_call(
        _mm_kernel,
        grid=(pl.cdiv(n_rows, tm), n // tn),
        in_specs=[
            pl.BlockSpec((tm, d), lambda i, j: (i, 0)),
            pl.BlockSpec((None, d, tn), lambda i, j: (layer, 0, blk0 + j)),
        ],
        out_specs=pl.BlockSpec((tm, tn), lambda i, j: (i, j)),
        out_shape=jax.ShapeDtypeStruct((n_rows, n), F32),
        compiler_params=_cparams(("parallel", "parallel")),
        name="mix_proj",
    )(xn, w_all)


def _attn_kernel(q_ref, k_ref, v_ref, o_ref, s0_ref, s1_ref, p_ref, alpha_ref, m_ref, l_ref,
                 acc_ref, *,
                 group, tq, dv, key_lo, key_hi):
    rows = group * tq
    q = q_ref[...].reshape(rows, q_ref.shape[-1])
    m_ref[...] = jnp.full(m_ref.shape, -jnp.inf, F32)
    l_ref[...] = jnp.zeros(l_ref.shape, F32)
    acc_ref[...] = jnp.zeros(acc_ref.shape, F32)

    def scores(off, width, s_ref):
        s_ref[:, 0:width] = jnp.dot(q, k_ref[:, pl.ds(off, width)], preferred_element_type=F32)

    def update(off, width, s_ref):
        for rb in range(rows // ATTN_ROW_BLOCK):
            sl = slice(rb * ATTN_ROW_BLOCK, (rb + 1) * ATTN_ROW_BLOCK)
            parts = [s_ref[sl, i * LANES:(i + 1) * LANES] for i in range(width // LANES)]
            m_prev = m_ref[sl, :]
            m_cur = functools.reduce(jnp.maximum, parts)
            m_new = jnp.maximum(m_prev, jnp.max(m_cur, axis=-1, keepdims=True))
            alpha = jnp.exp2(m_prev - m_new)
            p_parts = [jnp.exp2(part - m_new) for part in parts]
            l_ref[sl, :] = alpha * l_ref[sl, :] + functools.reduce(jnp.add, p_parts)
            for i, pp in enumerate(p_parts):
                p_ref[sl, i * LANES:(i + 1) * LANES] = pp.astype(BF16)
            alpha_ref[sl, :] = alpha
            m_ref[sl, :] = m_new
        pv = jnp.dot(p_ref[:, 0:width], v_ref[pl.ds(off, width), :],
                     preferred_element_type=F32)
        acc_ref[...] = alpha_ref[...] * acc_ref[...] + pv

    n_full, tail = divmod(key_hi - key_lo, TK)
    n_chunks = n_full + (1 if tail else 0)
    bufs = (s0_ref, s1_ref)

    def chunk(idx):
        return key_lo + idx * TK, (TK if idx < n_full else tail)

    per_trip = 2 * ATTN_PAIRS_PER_TRIP
    n_trips = max(0, (n_full - 1) // per_trip)
    scores(*chunk(0), s0_ref)

    def trip(i, carry):
        off = pl.multiple_of(key_lo + per_trip * i * TK, TK)
        for k in range(per_trip):
            scores(off + (k + 1) * TK, TK, bufs[(k + 1) % 2])
            update(off + k * TK, TK, bufs[k % 2])
        return carry

    lax.fori_loop(0, n_trips, trip, 0)
    for idx in range(per_trip * n_trips, n_chunks):
        if idx + 1 < n_chunks:
            scores(*chunk(idx + 1), bufs[(idx + 1) % 2])
        update(*chunk(idx), bufs[idx % 2])
    out = acc_ref[...] / jnp.sum(l_ref[...], axis=-1, keepdims=True)
    for g in range(group):
        o_ref[:, g * dv:(g + 1) * dv] = out[g * tq:(g + 1) * tq].astype(o_ref.dtype)


def _attn_call(q, k, v, group, tq, q_pos0, q_len, key_lo, key_hi):
    bsz, n_heads, p_len, dk = q.shape
    n_kv = k.shape[1]
    dv = v.shape[-1]
    assert dv == LANES and n_heads == n_kv * group
    assert q_pos0 % tq == 0 and q_len % tq == 0
    assert key_lo % TK == 0 and (key_hi - key_lo) % LANES == 0
    q_tiles = q_len // tq
    q_blk0 = q_pos0 // tq
    return pl.pallas_call(
        functools.partial(_attn_kernel, group=group, tq=tq, dv=dv, key_lo=key_lo, key_hi=key_hi),
        grid=(bsz, n_kv, q_tiles),
        in_specs=[
            pl.BlockSpec((None, group, tq, dk), lambda b, h, qi: (b, h, q_blk0 + qi, 0)),
            pl.BlockSpec((None, None, dk, p_len), lambda b, h, qi: (b, h, 0, 0)),
            pl.BlockSpec((None, None, p_len, dv), lambda b, h, qi: (b, h, 0, 0)),
        ],
        out_specs=pl.BlockSpec((tq, group * dv), lambda b, h, qi: (b * q_tiles + qi, h)),
        out_shape=jax.ShapeDtypeStruct((bsz * q_len, n_heads * dv), BF16),
        scratch_shapes=[
            pltpu.VMEM((group * tq, TK), F32),
            pltpu.VMEM((group * tq, TK), F32),
            pltpu.VMEM((group * tq, TK), BF16),
            pltpu.VMEM((group * tq, LANES), F32),
            pltpu.VMEM((group * tq, LANES), F32),
            pltpu.VMEM((group * tq, LANES), F32),
            pltpu.VMEM((group * tq, dv), F32),
        ],
        compiler_params=_cparams(("parallel", "parallel", "arbitrary")),
        name="attn_g%d_q%d" % (group, tq),
    )(q, k, v)


def _attention(q, k, v, group, tq_lat, geo, with_ctx_queries):
    seq, ctx_len, p_len = geo["seq"], geo["ctx"], geo["p_len"]
    out = _attn_call(q, k, v, group, tq_lat, 0, seq, 0, p_len)
    if with_ctx_queries:
        out_ctx = _attn_call(q, k, v, group, TM_A, seq, ctx_len, seq, p_len)
        out = jnp.concatenate([out, out_ctx], axis=0)
    return out


def _conv_a_kernel(cur_ref, prev_ref, next_ref, w_ref, vec_ref, o_ref, ext_ref, y_ref, *,
                   d_a, tm, first_of, last_of):
    i = pl.program_id(0)

    def glu(u):
        return u[:, :d_a] * _sigmoid(u[:, d_a:])

    keep_prev = jnp.where(first_of(i), 0.0, 1.0)
    keep_next = jnp.where(last_of(i), 0.0, 1.0)
    ext_ref[0, 0:HALO_A, :] = glu(prev_ref[...]) * keep_prev
    ext_ref[0, HALO_A:HALO_A + tm, :] = glu(cur_ref[...])
    ext_ref[0, HALO_A + tm:2 * HALO_A + tm, :] = glu(next_ref[...]) * keep_next
    span = tm + 2 * HALO_A - SUBLANES
    for r in range(1, SUBLANES):
        ext_ref[r, 0:span, :] = ext_ref[0, r:r + span, :]

    def lane_chunk(c, carry):
        cs = pl.multiple_of(c * CONV_LANES, CONV_LANES)
        for rc in range(tm // CONV_ROWS):
            acc = jnp.zeros((CONV_ROWS, CONV_LANES), F32)
            for tap in range(CONV_A):
                s = tap + HALO_A - CONV_A // 2
                q8, r = divmod(s, SUBLANES)
                row0 = q8 * SUBLANES + rc * CONV_ROWS
                acc = acc + (w_ref[tap:tap + 1, pl.ds(cs, CONV_LANES)]
                             * ext_ref[r, row0:row0 + CONV_ROWS, pl.ds(cs, CONV_LANES)])
            y_ref[rc * CONV_ROWS:(rc + 1) * CONV_ROWS, pl.ds(cs, CONV_LANES)] = acc
        return carry

    lax.fori_loop(0, d_a // CONV_LANES, lane_chunk, 0)
    y = y_ref[...] + vec_ref[0:1, :]
    mu = jnp.mean(y, axis=-1, keepdims=True)
    yc = y - mu
    var = jnp.mean(yc * yc, axis=-1, keepdims=True)
    z = yc * lax.rsqrt(var + EPS) * vec_ref[1:2, :] + vec_ref[2:3, :]
    o_ref[...] = (z * _sigmoid(z)).astype(o_ref.dtype)


def _conv_a_call(p2, w_dw, vecs, layer, n_rows, geo):
    d_a = w_dw.shape[2]
    tm = TM_A
    hb = tm // HALO_A
    n_halo_blocks = p2.shape[0] // HALO_A
    return pl.pallas_call(
        functools.partial(_conv_a_kernel, d_a=d_a, tm=tm,
                          first_of=geo["first_a"], last_of=geo["last_a"]),
        grid=(n_rows // tm,),
        in_specs=[
            pl.BlockSpec((tm, 2 * d_a), lambda i: (i, 0)),
            pl.BlockSpec((HALO_A, 2 * d_a), lambda i: (jnp.maximum(i * hb - 1, 0), 0)),
            pl.BlockSpec((HALO_A, 2 * d_a),
                         lambda i: (jnp.minimum((i + 1) * hb, n_halo_blocks - 1), 0)),
            pl.BlockSpec((None, CONV_A, d_a), lambda i: (layer, 0, 0)),
            pl.BlockSpec((None, SUBLANES, d_a), lambda i: (layer, 0, 0)),
        ],
        out_specs=pl.BlockSpec((tm, d_a), lambda i: (i, 0)),
        out_shape=jax.ShapeDtypeStruct((n_rows, d_a), BF16),
        scratch_shapes=[
            pltpu.VMEM((SUBLANES, tm + 2 * HALO_A, d_a), F32),
            pltpu.VMEM((tm, d_a), F32),
        ],
        compiler_params=_cparams(("parallel",)),
        name="conv_a",
    )(p2, p2, p2, w_dw, vecs)


def _conv_d_kernel(bg_ref, cg_ref, hh_ref, cg_prev_ref, hh_prev_ref, cg_next_ref, hh_next_ref,
                   w_ref, o_ref, ext_ref, *, tm, first_of, last_of):
    i = pl.program_id(0)
    keep_prev = jnp.where(first_of(i), 0.0, 1.0)
    keep_next = jnp.where(last_of(i), 0.0, 1.0)
    ext_ref[0:HALO_D, :] = cg_prev_ref[...] * hh_prev_ref[...] * keep_prev
    ext_ref[HALO_D:HALO_D + tm, :] = cg_ref[...] * hh_ref[...]
    ext_ref[HALO_D + tm:2 * HALO_D + tm, :] = cg_next_ref[...] * hh_next_ref[...] * keep_next
    acc = None
    for tap in range(CONV_D):
        s = tap + HALO_D - CONV_D // 2
        term = w_ref[tap:tap + 1, :] * ext_ref[s:s + tm, :]
        acc = term if acc is None else acc + term
    o_ref[...] = (bg_ref[...] * acc).astype(o_ref.dtype)


def _conv_d_call(p2, w_dw, layer, n_rows, col0, geo):
    d_d = w_dw.shape[2]
    tm = TM_A
    hb = tm // HALO_D
    n_halo_blocks = p2.shape[0] // HALO_D
    cb = col0 // d_d
    cur = lambda k: pl.BlockSpec((tm, d_d), lambda i: (i, cb + k))
    prev = lambda k: pl.BlockSpec((HALO_D, d_d), lambda i: (jnp.maximum(i * hb - 1, 0), cb + k))
    nxt = lambda k: pl.BlockSpec(
        (HALO_D, d_d), lambda i: (jnp.minimum((i + 1) * hb, n_halo_blocks - 1), cb + k))
    return pl.pallas_call(
        functools.partial(_conv_d_kernel, tm=tm, first_of=geo["first_a"], last_of=geo["last_a"]),
        grid=(n_rows // tm,),
        in_specs=[cur(0), cur(1), cur(2), prev(1), prev(2), nxt(1), nxt(2),
                  pl.BlockSpec((None, SUBLANES, d_d), lambda i: (layer, 0, 0))],
        out_specs=pl.BlockSpec((tm, d_d), lambda i: (i, 0)),
        out_shape=jax.ShapeDtypeStruct((n_rows, d_d), BF16),
        scratch_shapes=[pltpu.VMEM((tm + 2 * HALO_D, d_d), F32)],
        compiler_params=_cparams(("parallel",)),
        name="conv_d",
    )(p2, p2, p2, p2, p2, p2, p2, w_dw)


def _merge_kernel(h_ref, mod_ref, xn_ref, aa_ref, ab_ref, ac_ref, ad_ref,
                  wg0_ref, wg1_ref, wg2_ref, wg3_ref, wa_ref, wb_ref, wc_ref, wd_ref,
                  wo_ref, o_ref):
    j = pl.program_id(1)

    @pl.when(j == 0)
    def _():
        o_ref[...] = jnp.zeros(o_ref.shape, F32)

    xn = xn_ref[...]
    y = None
    for a_ref, wg_ref, wbr_ref in ((aa_ref, wg0_ref, wa_ref), (ab_ref, wg1_ref, wb_ref),
                                   (ac_ref, wg2_ref, wc_ref), (ad_ref, wg3_ref, wd_ref)):
        gl = jnp.dot(xn, wg_ref[...], preferred_element_type=F32)
        yb = jnp.dot(a_ref[...], wbr_ref[...], preferred_element_type=F32)
        term = _sigmoid(gl) * yb
        y = term if y is None else y + term
    o_ref[...] += jnp.dot(y.astype(BF16), wo_ref[...], preferred_element_type=F32)

    @pl.when(j == pl.num_programs(1) - 1)
    def _():
        o_ref[...] = h_ref[...] + mod_ref[5:6, :] * o_ref[...]


def _merge_call(h, mods, xn, acts, w_gates, gate_col0, w_branches, w_o, layer, n_rows, group_of):
    d = h.shape[1]
    tn = _pick_tile(d, TN_MERGE)
    n_j = d // tn
    assert gate_col0 % tn == 0
    g0 = gate_col0 // tn
    d_br = w_branches[0].shape[1]
    tm = TM_F
    act = pl.BlockSpec((tm, d_br), lambda i, j: (i, 0))
    gate = lambda b: pl.BlockSpec((None, d, tn), lambda i, j: (layer, 0, g0 + b * n_j + j))
    branch = pl.BlockSpec((None, d_br, tn), lambda i, j: (layer, 0, j))
    return pl.pallas_call(
        _merge_kernel,
        grid=(n_rows // tm, n_j),
        in_specs=[
            pl.BlockSpec((tm, d), lambda i, j: (i, 0)),
            pl.BlockSpec((None, None, N_MOD, d), lambda i, j: (layer, group_of(i), 0, 0)),
            pl.BlockSpec((tm, d), lambda i, j: (i, 0)),
            act, act, act, act,
            gate(0), gate(1), gate(2), gate(3),
            branch, branch, branch, branch,
            pl.BlockSpec((None, tn, d), lambda i, j: (layer, j, 0)),
        ],
        out_specs=pl.BlockSpec((tm, d), lambda i, j: (i, 0)),
        out_shape=jax.ShapeDtypeStruct((n_rows, d), F32),
        compiler_params=_cparams(("parallel", "arbitrary")),
        name="merge",
    )(h, mods, xn, *acts, w_gates, w_gates, w_gates, w_gates, *w_branches, w_o)


def _rope_tables(seq, ctx):
    t = jnp.arange(seq)
    row = (t // GRID_W).astype(F32)
    col = (t % GRID_W).astype(F32)

    def cos_sin(d_rot):
        n_freq = d_rot // 4
        inv = ROPE_BASE ** (-jnp.arange(n_freq, dtype=F32) / n_freq)
        ang = jnp.concatenate([row[:, None] * inv, col[:, None] * inv], axis=-1)
        pad = ((0, ctx), (0, 0))
        return (jnp.pad(jnp.cos(ang), pad, constant_values=1.0), jnp.pad(jnp.sin(ang), pad))

    cb, sb = cos_sin(ROPE_B)
    cc, sc = cos_sin(DH_C)
    zb = jnp.zeros_like(sb)
    fill = jnp.zeros((seq + ctx, LANES - ROPE_B), F32)
    return jnp.stack([
        jnp.concatenate([cb, cb, fill], axis=-1),
        jnp.concatenate([zb, sb, fill], axis=-1),
        jnp.concatenate([-sb, zb, fill], axis=-1),
        jnp.concatenate([cc, cc], axis=-1),
        jnp.concatenate([-sc, sc], axis=-1),
    ])


def kernel(x, c, ctx, c_ctx, w_mod, b_mod, w_ffn1_in, w_ffn1_out, w_ffn2_in, w_ffn2_out,
           w_in, g_q_lora, w_uq, g_kv_lora, w_ukv, g_q_b, g_k_b, w_o_b, g_q_c, g_k_c, w_o_c,
           w_dw_a, b_dw_a, g_ln_a, b_ln_a, w_out_a, w_dw_d, w_out_d, w_o):
    bsz, seq, d = x.shape
    ctx_len = ctx.shape[1]
    depth = w_mod.shape[0]
    d_ff = w_ffn1_out.shape[1]
    ql, kvl = g_q_lora.shape[1], g_kv_lora.shape[1]
    d_a, d_d = w_dw_a.shape[2], w_dw_d.shape[2]
    n_lat = bsz * seq
    t_rows = n_lat + bsz * ctx_len
    p_len = seq + ctx_len
    assert seq % TM_F == 0 and (bsz * ctx_len) % TM_F == 0 and ctx_len % TM_A == 0
    assert seq % TK == 0 and ctx_len % LANES == 0 and bsz + 1 <= SUBLANES
    assert d_a == d_d == HB * V_B == HC * DH_C

    tps_a, tps_f = seq // TM_A, seq // TM_F
    nlt_a = bsz * tps_a
    ncb_a = ctx_len // TM_A
    geo = dict(
        bsz=bsz, seq=seq, ctx=ctx_len, p_len=p_len, n_lat=n_lat, t_rows=t_rows,
        group_a=lambda i: jnp.minimum(i // tps_a, bsz),
        batch_of=lambda i: jnp.where(i < nlt_a, i // tps_a, (i - nlt_a) // ncb_a),
        pos_of=lambda i: jnp.where(i < nlt_a, i % tps_a, tps_a + (i - nlt_a) % ncb_a),
        first_a=lambda i: jnp.where(i < nlt_a, i % tps_a == 0, (i - nlt_a) % ncb_a == 0),
        last_a=lambda i: jnp.where(i < nlt_a, i % tps_a == tps_a - 1,
                                   (i - nlt_a) % ncb_a == ncb_a - 1),
    )
    group_f = lambda i: jnp.minimum(i // tps_f, bsz)
    tm_ffn = min(TM_FFN, seq)
    assert seq % tm_ffn == 0 and bsz * ctx_len <= tm_ffn
    group_ffn = lambda i: jnp.minimum(i // (seq // tm_ffn), bsz)

    c_rows = jnp.concatenate(
        [c, c_ctx[None, :], jnp.zeros((SUBLANES - bsz - 1, d), F32)], axis=0)
    mods = _mod_call(c_rows, w_mod, b_mod).reshape(depth, SUBLANES, N_MOD, d)

    w1_in, w2_in = w_ffn1_in.astype(BF16), w_ffn2_in.astype(BF16)
    w1_out, w2_out = w_ffn1_out.astype(BF16), w_ffn2_out.astype(BF16)

    kv_cols = kvl + ROPE_B + 2 * HKV_C * DH_C
    attn_cols = kv_cols + ql + HC * DH_C
    mix_cols = 2 * d_a + 3 * d_d
    head_cols = kvl + ROPE_B
    n_head = _round_up(head_cols, LANES)
    n_a = attn_cols - head_cols
    mix0, gate0 = n_a, n_a + mix_cols
    assert n_a % _pick_tile(mix_cols, TN_MIX) == 0 and gate0 % _pick_tile(d, TN_MERGE) == 0
    w_head = jnp.pad(w_in[:, :, :head_cols].astype(BF16),
                     ((0, 0), (0, 0), (0, n_head - head_cols)))
    w_all = w_in[:, :, head_cols:].astype(BF16)
    w_branches = [w.astype(BF16) for w in (w_out_a, w_o_b, w_o_c, w_out_d)]
    w_o_bf = w_o.astype(BF16)

    w_ukv_r = w_ukv.astype(BF16).reshape(depth, kvl, HB, 2, NOPE_B)
    w_ukv_r = jnp.transpose(w_ukv_r, (0, 1, 3, 2, 4)).reshape(depth, kvl, 2 * HB * NOPE_B)
    w_uq_r = jnp.pad(w_uq.astype(BF16).reshape(depth, ql, HB, QK_B),
                     ((0, 0), (0, 0), (0, 0), (0, 2 * LANES - QK_B)))
    w_uq_r = w_uq_r.reshape(depth, ql, HB * 2 * LANES)

    gw = max(kvl, ql, 4 * LANES)
    padw = lambda v: jnp.pad(v, ((0, 0), (0, gw - v.shape[1])))
    pad_pe = lambda g: jnp.pad(g[:, NOPE_B:], ((0, 0), (0, LANES - ROPE_B)))
    gvec = jnp.stack([
        padw(g_kv_lora), padw(g_q_lora),
        padw(jnp.concatenate([g_k_b[:, :NOPE_B], pad_pe(g_k_b),
                              g_q_b[:, :NOPE_B], pad_pe(g_q_b)], axis=-1)),
        padw(jnp.concatenate([g_k_c, g_q_c], axis=-1)),
    ] + [jnp.zeros((depth, gw), F32)] * (SUBLANES - 4), axis=1)
    vecs_a = jnp.stack([b_dw_a, g_ln_a, b_ln_a]
                       + [jnp.zeros_like(b_dw_a)] * (SUBLANES - 3), axis=1)
    w_dw_d_p = jnp.pad(w_dw_d, ((0, 0), (0, SUBLANES - CONV_D), (0, 0)))
    tabs = _rope_tables(seq, ctx_len)

    h = jnp.concatenate([x.reshape(n_lat, d), ctx.reshape(bsz * ctx_len, d)], axis=0)
    for layer in range(depth):
        last = layer == depth - 1
        rows_out = n_lat if last else t_rows
        h = _ffn_call(h, mods, w1_in, w1_out, layer, 0, t_rows, tm_ffn, group_ffn)
        xn, qb, kb, vb, qc, kc, vc = _prep_call(
            h, mods, w_head, w_all, w_ukv_r, w_uq_r, gvec, tabs, layer, geo, n_a)
        p2 = _mix_proj_call(xn, w_all, layer, t_rows, mix0, mix_cols)
        ab = _attention(qb, kb, vb, 1, min(TQ_ROWS, seq), geo, not last)
        ac = _attention(qc, kc, vc, GROUP_C, min(TQ_ROWS // GROUP_C, seq), geo, not last)
        aa = _conv_a_call(p2, w_dw_a, vecs_a, layer, rows_out, geo)
        ad = _conv_d_call(p2, w_dw_d_p, layer, rows_out, 2 * d_a, geo)
        h = _merge_call(h, mods, xn, (aa, ab, ac, ad), w_all, gate0, w_branches, w_o_bf,
                        layer, rows_out, group_f)
        h = _ffn_call(h, mods, w2_in, w2_out, layer, 6, rows_out, tm_ffn, group_ffn)
    return h.reshape(bsz, seq, d)
```
